```python
import math
import numpy as np
import jax
import jax.numpy as jnp
from jax import lax

D_MODEL = 4096
BATCH = 2
SEQ = 8192
DEPTH = 2

N_META = 16
GROUP_WIDTH = D_MODEL // 2
D_FF = 4 * D_MODEL
NORM_EPS = 1e-6
Q_BLOCK = 128
ROPE_THETA = 500000.0

LRU_WIDTH = GROUP_WIDTH
LRU_BLOCKS = 16
LRU_BLOCK_DIM = LRU_WIDTH // LRU_BLOCKS
CONV_WIDTH = 4
LRU_C = 8.0

DIFF_HEAD_DIM = 128
DIFF_HEADS = GROUP_WIDTH // (2 * DIFF_HEAD_DIM)
ROT_DIM = DIFF_HEAD_DIM // 4

RWKV_HEAD_DIM = 64
RWKV_HEADS = GROUP_WIDTH // RWKV_HEAD_DIM
DECAY_LORA = max(32, round(D_MODEL ** 0.5 * 1.8 / 32) * 32)
AAA_LORA = max(32, round(D_MODEL ** 0.5 * 1.8 / 32) * 32)
GATE_LORA = max(32, round(D_MODEL ** 0.8 * 0.6 / 32) * 32)
RWKV_GN_EPS = 64e-5

FOX_HEAD_DIM = 128
FOX_HEADS = GROUP_WIDTH // FOX_HEAD_DIM

EVEN_SIZES = (LRU_WIDTH, LRU_WIDTH, 2 * DIFF_HEADS * DIFF_HEAD_DIM,
              2 * DIFF_HEADS * DIFF_HEAD_DIM, 2 * DIFF_HEADS * DIFF_HEAD_DIM)
RWKV_SIZES = (GROUP_WIDTH, GROUP_WIDTH, GROUP_WIDTH, DECAY_LORA, AAA_LORA, GATE_LORA)
RWKV_SLAB = sum(RWKV_SIZES)
ODD_SIZES = (RWKV_SLAB, FOX_HEADS * FOX_HEAD_DIM, FOX_HEADS * FOX_HEAD_DIM,
             FOX_HEADS * FOX_HEAD_DIM, FOX_HEADS)

F32 = jnp.float32

kernel_name = "hybrid_lru_diffattn_rwkv7_fox_trunk"


def _split(u, sizes):
    cuts = [int(c) for c in np.cumsum(sizes)[:-1]]
    return jnp.split(u, cuts, axis=-1)


def _rms_norm(x, gain, eps=NORM_EPS):
    x32 = x.astype(F32)
    y = x32 * lax.rsqrt(jnp.mean(x32 * x32, axis=-1, keepdims=True) + eps)
    return (y * gain.astype(F32)).astype(x.dtype)


def _partial_rotary(x, pos):
    half = ROT_DIM // 2
    inv_freq = ROPE_THETA ** (-jnp.arange(half, dtype=F32) / half)
    ang = pos.astype(F32)[:, None] * inv_freq[None, :]
    bshape = (1, pos.shape[0]) + (1,) * (x.ndim - 3) + (half,)
    cos = jnp.cos(ang).reshape(bshape)
    sin = jnp.sin(ang).reshape(bshape)
    xr = x[..., :ROT_DIM].astype(F32)
    x1, x2 = xr[..., :half], xr[..., half:]
    rot = jnp.concatenate([x1 * cos - x2 * sin, x2 * cos + x1 * sin], axis=-1).astype(x.dtype)
    return jnp.concatenate([rot, x[..., ROT_DIM:]], axis=-1)


def _causal_block_sweep(attend, q_args):
    B, T = q_args[0].shape[:2]
    n_blk = (T - N_META) // Q_BLOCK
    meta_out = attend(tuple(q[:, :N_META] for q in q_args), jnp.arange(N_META))

    def to_blocks(q):
        r = q[:, N_META:].reshape((B, n_blk, Q_BLOCK) + q.shape[2:])
        return jnp.moveaxis(r, 1, 0)

    blocks = tuple(to_blocks(q) for q in q_args)
    starts = N_META + Q_BLOCK * jnp.arange(n_blk)
    out = lax.map(lambda bs: attend(bs[0], bs[1] + jnp.arange(Q_BLOCK)), (blocks, starts))
    out = jnp.moveaxis(out, 0, 1).reshape((B, n_blk * Q_BLOCK) + out.shape[3:])
    return jnp.concatenate([meta_out, out], axis=1)


def _linear_combine(left, right):
    a1, b1 = left
    a2, b2 = right
    return a1 * a2, a2 * b1 + b2


def _rglru_branch(xb, gb, conv_w, conv_b, w_a, b_a, w_x, b_x, lam):
    B, T, W = xb.shape
    xp = jnp.pad(xb, ((0, 0), (CONV_WIDTH - 1, 0), (0, 0)))
    xc = conv_b + sum(xp[:, j:j + T] * conv_w[j] for j in range(CONV_WIDTH))
    xh = xc.reshape(B, T, LRU_BLOCKS, LRU_BLOCK_DIM)
    r = jax.nn.sigmoid((jnp.einsum('btnc,ncd->btnd', xh, w_a) + b_a).astype(F32)).reshape(B, T, W)
    i = jax.nn.sigmoid((jnp.einsum('btnc,ncd->btnd', xh, w_x) + b_x).astype(F32)).reshape(B, T, W)
    log_a = -LRU_C * r * jax.nn.softplus(-lam.astype(F32))
    a = jnp.exp(log_a)
    u = jnp.sqrt(-jnp.expm1(2.0 * log_a)) * (i * xc.astype(F32))
    _, h = lax.associative_scan(_linear_combine, (a, u), axis=1)
    return h.astype(xb.dtype) * jax.nn.gelu(gb)


def _diff_lambda_init(layer):
    return 0.8 - 0.6 * math.exp(-0.3 * layer)


def _diff_attention_branch(q, k, v, pos, q_gain, k_gain, lq1, lk1, lq2, lk2, sub_gain, lambda_init):
    B, T, _ = q.shape
    q = _partial_rotary(_rms_norm(q.reshape(B, T, DIFF_HEADS, 2, DIFF_HEAD_DIM), q_gain), pos)
    k = _partial_rotary(_rms_norm(k.reshape(B, T, DIFF_HEADS, 2, DIFF_HEAD_DIM), k_gain), pos)
    v = v.reshape(B, T, DIFF_HEADS, 2 * DIFF_HEAD_DIM)
    k1, k2 = k[..., 0, :], k[..., 1, :]
    lam = (jnp.exp(jnp.sum(lq1.astype(F32) * lk1.astype(F32)))
           - jnp.exp(jnp.sum(lq2.astype(F32) * lk2.astype(F32))) + lambda_init)
    scale = DIFF_HEAD_DIM ** -0.5
    kpos = jnp.arange(T)

    def attend(qb, qpos):
        q1b, q2b = qb
        mask = kpos[None, :] <= qpos[:, None]

        def probs(qx, kx):
            s = jnp.einsum('bqhd,bkhd->bhqk', qx, kx).astype(F32) * scale
            return jax.nn.softmax(jnp.where(mask, s, -jnp.inf), axis=-1)

        p = probs(q1b, k1) - lam * probs(q2b, k2)
        return jnp.einsum('bhqk,bkhe->bqhe', p.astype(v.dtype), v)

    o = _causal_block_sweep(attend, (q[..., 0, :], q[..., 1, :]))
    o = _rms_norm(o, sub_gain) * (1.0 - lambda_init)
    return o.reshape(B, T, GROUP_WIDTH)


def _rwkv7_step(S, inp):
    r_t, w_t, k_t, v_t, a_t, b_t = inp
    sa = jnp.einsum('bhij,bhj->bhi', S, a_t)
    S = S * w_t[:, :, None, :] + sa[..., None] * b_t[:, :, None, :] + v_t[..., None] * k_t[:, :, None, :]
    y = jnp.einsum('bhij,bhj->bhi', S, r_t)
    return S, y


def _rwkv7_branch(z, mu, w0, w_up, a0, a_up, g_up, k_k, k_a, r_k, gn_w, gn_b):
    B, T, _ = z.shape
    z_prev = jnp.pad(z, ((0, 0), (1, 0), (0, 0)))[:, :T]
    z = z + (z_prev - z) * mu
    r, k, v, wd, ad, gd = _split(z, RWKV_SIZES)

    def heads(t):
        return t.reshape(B, T, RWKV_HEADS, RWKV_HEAD_DIM)

    w_log = -jax.nn.softplus(-(w0 + jnp.tanh(wd) @ w_up).astype(F32)) - 0.5
    decay = jnp.exp(-jnp.exp(w_log))
    a = jax.nn.sigmoid((a0 + ad @ a_up).astype(F32))
    g = jax.nn.sigmoid(gd) @ g_up
    kk = heads((k * k_k).astype(F32))
    kk = kk / jnp.maximum(jnp.linalg.norm(kk, axis=-1, keepdims=True), 1e-12)
    k32 = heads(k.astype(F32) * (1.0 + (a - 1.0) * k_a.astype(F32)))
    r32 = heads(r.astype(F32))
    v32 = heads(v.astype(F32))
    a_h = heads(a)

    def tm(t):
        return jnp.moveaxis(t, 1, 0)

    xs = (tm(r32), tm(heads(decay)), tm(k32), tm(v32), tm(-kk), tm(kk * a_h))
    S0 = jnp.zeros((B, RWKV_HEADS, RWKV_HEAD_DIM, RWKV_HEAD_DIM), F32)
    _, y = lax.scan(_rwkv7_step, S0, xs)
    y = jnp.moveaxis(y, 0, 1)
    mean = jnp.mean(y, axis=-1, keepdims=True)
    var = jnp.mean(jnp.square(y - mean), axis=-1, keepdims=True)
    y = ((y - mean) * lax.rsqrt(var + RWKV_GN_EPS)).reshape(B, T, GROUP_WIDTH) * gn_w + gn_b
    bonus = jnp.sum(r32 * k32 * r_k.astype(F32), axis=-1, keepdims=True) * v32
    out = (y + bonus.reshape(B, T, GROUP_WIDTH)) * g.astype(F32)
    return out.astype(z.dtype)


def _forgetting_attention_branch(q, k, v, f_logit, f_bias, q_gain, k_gain):
    B, T, _ = q.shape
    q = _rms_norm(q.reshape(B, T, FOX_HEADS, FOX_HEAD_DIM), q_gain)
    k = _rms_norm(k.reshape(B, T, FOX_HEADS, FOX_HEAD_DIM), k_gain)
    v = v.reshape(B, T, FOX_HEADS, FOX_HEAD_DIM)
    log_f = jax.nn.log_sigmoid((f_logit + f_bias).astype(F32))
    c = jnp.cumsum(log_f, axis=1)
    c_k = jnp.transpose(c, (0, 2, 1))
    scale = FOX_HEAD_DIM ** -0.5
    kpos = jnp.arange(T)

    def attend(qb, qpos):
        q_b, c_q = qb
        mask = kpos[None, :] <= qpos[:, None]
        s = (jnp.einsum('bqhd,bkhd->bhqk', q_b, k).astype(F32) * scale
             + jnp.transpose(c_q, (0, 2, 1))[..., None] - c_k[:, :, None, :])
        p = jax.nn.softmax(jnp.where(mask, s, -jnp.inf), axis=-1)
        return jnp.einsum('bhqk,bkhe->bqhe', p.astype(v.dtype), v)

    o = _causal_block_sweep(attend, (q, c))
    return o.reshape(B, T, GROUP_WIDTH)


def _even_mixer(hn, pos, lambda_init, w_in, conv_w, conv_b, lru_wa, lru_ba, lru_wx, lru_bx, lru_lam,
                q_gain, k_gain, lq1, lk1, lq2, lk2, sub_gain, w_out):
    xb, gb, q, k, v = _split(hn @ w_in, EVEN_SIZES)
    y_a = _rglru_branch(xb, gb, conv_w, conv_b, lru_wa, lru_ba, lru_wx, lru_bx, lru_lam)
    y_b = _diff_attention_branch(q, k, v, pos, q_gain, k_gain, lq1, lk1, lq2, lk2, sub_gain, lambda_init)
    return jnp.concatenate([y_a, y_b], axis=-1) @ w_out


def _odd_mixer(hn, w_in, mu, w0, w_up, a0, a_up, g_up, k_k, k_a, r_k, gn_w, gn_b,
               fox_q_gain, fox_k_gain, fox_f_bias, w_out):
    z, q, k, v, f = _split(hn @ w_in, ODD_SIZES)
    y_c = _rwkv7_branch(z, mu, w0, w_up, a0, a_up, g_up, k_k, k_a, r_k, gn_w, gn_b)
    y_d = _forgetting_attention_branch(q, k, v, f, fox_f_bias, fox_q_gain, fox_k_gain)
    return jnp.concatenate([y_c, y_d], axis=-1) @ w_out


def _squared_relu_mlp(hn, w_up, w_down):
    return jnp.square(jax.nn.relu(hn @ w_up)) @ w_down


def setup_inputs(seed: int = 0) -> dict:
    key = jax.random.key(seed)
    ks = iter(jax.random.split(key, 48))
    d = D_MODEL

    def nrm(shape, scale):
        return jax.random.normal(next(ks), shape, F32) * scale

    def gain(n):
        return 1.0 + nrm((n,), 0.02)

    def unif(shape, lo, hi):
        return jax.random.uniform(next(ks), shape, F32, lo, hi)

    def lru_lambda():
        a_pow = unif((LRU_WIDTH,), 0.9, 0.999)
        a_base = a_pow ** (1.0 / LRU_C)
        return jnp.log(a_base) - jnp.log1p(-a_base)

    return {
        'x': nrm((BATCH, SEQ, d), 1.0),
        'meta_tokens': nrm((N_META, d), 1.0),
        'mix_norm_0': gain(d),
        'w_in_0': nrm((d, sum(EVEN_SIZES)), d ** -0.5),
        'conv_w_0': nrm((CONV_WIDTH, LRU_WIDTH), CONV_WIDTH ** -0.5),
        'conv_b_0': nrm((LRU_WIDTH,), 0.01),
        'lru_wa_0': nrm((LRU_BLOCKS, LRU_BLOCK_DIM, LRU_BLOCK_DIM), LRU_BLOCK_DIM ** -0.5),
        'lru_ba_0': nrm((LRU_BLOCKS, LRU_BLOCK_DIM), 0.01),
        'lru_wx_0': nrm((LRU_BLOCKS, LRU_BLOCK_DIM, LRU_BLOCK_DIM), LRU_BLOCK_DIM ** -0.5),
        'lru_bx_0': nrm((LRU_BLOCKS, LRU_BLOCK_DIM), 0.01),
        'lru_lam_0': lru_lambda(),
        'diff_q_gain_0': gain(DIFF_HEAD_DIM),
        'diff_k_gain_0': gain(DIFF_HEAD_DIM),
        'diff_lq1_0': nrm((DIFF_HEAD_DIM,), 0.1),
        'diff_lk1_0': nrm((DIFF_HEAD_DIM,), 0.1),
        'diff_lq2_0': nrm((DIFF_HEAD_DIM,), 0.1),
        'diff_lk2_0': nrm((DIFF_HEAD_DIM,), 0.1),
        'diff_sub_gain_0': gain(2 * DIFF_HEAD_DIM),
        'w_out_0': nrm((2 * GROUP_WIDTH, d), (2 * GROUP_WIDTH) ** -0.5),
        'ffn_norm_0': gain(d),
        'ffn_up_0': nrm((d, D_FF), d ** -0.5),
        'ffn_down_0': nrm((D_FF, d), D_FF ** -0.5),
        'mix_norm_1': gain(d),
        'w_in_1': nrm((d, sum(ODD_SIZES)), d ** -0.5),
        'rwkv_mu_1': unif((RWKV_SLAB,), 0.0, 1.0),
        'rwkv_w0_1': unif((GROUP_WIDTH,), -6.0, -1.0),
        'rwkv_w_up_1': nrm((DECAY_LORA, GROUP_WIDTH), 0.5 * DECAY_LORA ** -0.5),
        'rwkv_a0_1': nrm((GROUP_WIDTH,), 0.1),
        'rwkv_a_up_1': nrm((AAA_LORA, GROUP_WIDTH), 0.5 * AAA_LORA ** -0.5),
        'rwkv_g_up_1': nrm((GATE_LORA, GROUP_WIDTH), GATE_LORA ** -0.5),
        'rwkv_k_k_1': 0.85 + nrm((GROUP_WIDTH,), 0.02),
        'rwkv_k_a_1': gain(GROUP_WIDTH),
        'rwkv_r_k_1': nrm((RWKV_HEADS, RWKV_HEAD_DIM), 0.1),
        'rwkv_gn_w_1': gain(GROUP_WIDTH),
        'rwkv_gn_b_1': nrm((GROUP_WIDTH,), 0.01),
        'fox_q_gain_1': gain(FOX_HEAD_DIM),
        'fox_k_gain_1': gain(FOX_HEAD_DIM),
        'fox_f_bias_1': unif((FOX_HEADS,), 1.0, 4.0),
        'w_out_1': nrm((2 * GROUP_WIDTH, d), (2 * GROUP_WIDTH) ** -0.5),
        'ffn_norm_1': gain(d),
        'ffn_up_1': nrm((d, D_FF), d ** -0.5),
        'ffn_down_1': nrm((D_FF, d), D_FF ** -0.5),
    }


def reference(x, meta_tokens,
              mix_norm_0, w_in_0, conv_w_0, conv_b_0, lru_wa_0, lru_ba_0, lru_wx_0, lru_bx_0, lru_lam_0,
              diff_q_gain_0, diff_k_gain_0, diff_lq1_0, diff_lk1_0, diff_lq2_0, diff_lk2_0, diff_sub_gain_0,
              w_out_0, ffn_norm_0, ffn_up_0, ffn_down_0,
              mix_norm_1, w_in_1, rwkv_mu_1, rwkv_w0_1, rwkv_w_up_1, rwkv_a0_1, rwkv_a_up_1, rwkv_g_up_1,
              rwkv_k_k_1, rwkv_k_a_1, rwkv_r_k_1, rwkv_gn_w_1, rwkv_gn_b_1,
              fox_q_gain_1, fox_k_gain_1, fox_f_bias_1, w_out_1, ffn_norm_1, ffn_up_1, ffn_down_1):
    B = x.shape[0]
    meta = jnp.broadcast_to(meta_tokens[None].astype(x.dtype), (B, N_META, D_MODEL))
    h = jnp.concatenate([meta, x], axis=1)
    pos = jnp.arange(h.shape[1])

    even_params = (w_in_0, conv_w_0, conv_b_0, lru_wa_0, lru_ba_0, lru_wx_0, lru_bx_0, lru_lam_0,
                   diff_q_gain_0, diff_k_gain_0, diff_lq1_0, diff_lk1_0, diff_lq2_0, diff_lk2_0,
                   diff_sub_gain_0, w_out_0)
    odd_params = (w_in_1, rwkv_mu_1, rwkv_w0_1, rwkv_w_up_1, rwkv_a0_1, rwkv_a_up_1, rwkv_g_up_1,
                  rwkv_k_k_1, rwkv_k_a_1, rwkv_r_k_1, rwkv_gn_w_1, rwkv_gn_b_1,
                  fox_q_gain_1, fox_k_gain_1, fox_f_bias_1, w_out_1)
    mix_norms = (mix_norm_0, mix_norm_1)
    ffn_params = ((ffn_norm_0, ffn_up_0, ffn_down_0), (ffn_norm_1, ffn_up_1, ffn_down_1))

    for layer in range(DEPTH):
        hn = _rms_norm(h, mix_norms[layer])
        if layer % 2 == 0:
            h = h + _even_mixer(hn, pos, _diff_lambda_init(layer), *even_params)
        else:
            h = h + _odd_mixer(hn, *odd_params)
        f_gain, f_up, f_down = ffn_params[layer]
        h = h + _squared_relu_mlp(_rms_norm(h, f_gain), f_up, f_down)
    return h[:, N_META:]
```

```python
import functools
import math

import numpy as np
import jax
import jax.numpy as jnp
from jax import lax
from jax.experimental import pallas as pl
from jax.experimental.pallas import tpu as pltpu

F32 = jnp.float32
BF16 = jnp.bfloat16

D_MODEL = 4096
N_META = 16
GROUP_WIDTH = D_MODEL // 2
D_FF = 4 * D_MODEL
NORM_EPS = 1e-6
ROPE_THETA = 500000.0

LRU_BLOCK_DIM = 128
CONV_WIDTH = 4
LRU_C = 8.0

DIFF_HEAD_DIM = 128
DIFF_HEADS = GROUP_WIDTH // (2 * DIFF_HEAD_DIM)
ROT_HALF = DIFF_HEAD_DIM // 8

RWKV_HEAD_DIM = 64
RWKV_CHUNK = 64
DECAY_LORA = 128
AAA_LORA = 128
GATE_LORA = 480
RWKV_LORA_PAD = 1024
RWKV_GN_EPS = 64e-5

FOX_HEAD_DIM = 128
FOX_HEADS = GROUP_WIDTH // FOX_HEAD_DIM

LANES = 128
SUBLANES = 8
VMEM_CAP_BYTES = 56 * 1024 * 1024
MASK_VALUE = -1e30


def _tile(n, prefs):
    for p in prefs:
        if n % p == 0:
            return p
    raise ValueError(f"no tile of {prefs} divides {n}")


def _vmem_limit(block_bytes):
    return int(min(VMEM_CAP_BYTES, 2 * block_bytes + (8 << 20)))


def _nt(a, b):
    return lax.dot_general(a, b, (((1,), (1,)), ((), ())), preferred_element_type=F32)


def _tn(a, b):
    return lax.dot_general(a, b, (((0,), (0,)), ((), ())), preferred_element_type=F32)


def _dot(a, b):
    return jnp.dot(a, b, preferred_element_type=F32)


def _split2(x):
    hi = x.astype(BF16)
    lo = (x - hi.astype(F32)).astype(BF16)
    return hi, lo


def _split3(x):
    hi = x.astype(BF16)
    r1 = x - hi.astype(F32)
    mid = r1.astype(BF16)
    lo = (r1 - mid.astype(F32)).astype(BF16)
    return hi, mid, lo


def _dot3(a, b):
    ah, al = _split2(a)
    bh, bl = _split2(b)
    return _dot(ah, bh) + _dot(ah, bl) + _dot(al, bh)


def _softplus(x):
    return jnp.maximum(x, 0.0) + jnp.log1p(jnp.exp(-jnp.abs(x)))


def _rmsnorm_kernel(x_ref, g_ref, o_ref):
    x = x_ref[...]
    y = x * lax.rsqrt(jnp.mean(x * x, axis=-1, keepdims=True) + NORM_EPS)
    o_ref[...] = (y * g_ref[...]).astype(o_ref.dtype)


def _rmsnorm_rows(x, gain):
    m, d = x.shape
    tr = _tile(m, (320, 256, 128))
    return pl.pallas_call(
        _rmsnorm_kernel,
        grid=(m // tr,),
        in_specs=[pl.BlockSpec((tr, d), lambda i: (i, 0)),
                  pl.BlockSpec((1, d), lambda i: (0, 0))],
        out_specs=pl.BlockSpec((tr, d), lambda i: (i, 0)),
        out_shape=jax.ShapeDtypeStruct((m, d), BF16),
        compiler_params=pltpu.CompilerParams(
            dimension_semantics=("parallel",),
            vmem_limit_bytes=_vmem_limit(tr * d * 6)),
        name="rmsnorm_rows",
    )(x, gain.reshape(1, d).astype(F32))


def _mm_kernel(*refs, n_a, k_sizes, nk, has_res, relu2):
    a_refs = refs[:n_a]
    w_ref = refs[n_a]
    pos = n_a + 1
    res_ref = None
    if has_res:
        res_ref = refs[pos]
        pos += 1
    o_ref = refs[pos]
    acc_ref = refs[pos + 1] if nk > 1 else None

    def partial_product():
        acc = None
        off = 0
        for a_ref, ks in zip(a_refs, k_sizes):
            p = _dot(a_ref[...], w_ref[off:off + ks, :])
            acc = p if acc is None else acc + p
            off += ks
        return acc

    def finish(acc):
        if relu2:
            acc = jnp.square(jnp.maximum(acc, 0.0))
        if has_res:
            acc = acc + res_ref[...]
        o_ref[...] = acc.astype(o_ref.dtype)

    if nk == 1:
        finish(partial_product())
        return

    k = pl.program_id(2)

    @pl.when(k == 0)
    def _():
        acc_ref[...] = jnp.zeros_like(acc_ref)

    acc_ref[...] += partial_product()

    @pl.when(k == nk - 1)
    def _():
        finish(acc_ref[...])


def _matmul(a_list, w, *, n_off, n, tm, tn, tk=None, out_dtype=F32, residual=None, relu2=False, name):
    m = a_list[0].shape[0]
    k_sizes = tuple(a.shape[1] for a in a_list)
    k_total = sum(k_sizes)
    if tk is None:
        tk = k_total
    assert len(a_list) == 1 or tk == k_total
    assert m % tm == 0 and n % tn == 0 and n_off % tn == 0 and k_total % tk == 0
    nk = k_total // tk
    j_off = n_off // tn
    blk_k = (tk,) if len(a_list) == 1 else k_sizes
    in_specs = [pl.BlockSpec((tm, bk), lambda i, j, k: (i, k)) for bk in blk_k]
    in_specs.append(pl.BlockSpec((tk, tn), lambda i, j, k: (k, j + j_off)))
    args = list(a_list) + [w]
    if residual is not None:
        in_specs.append(pl.BlockSpec((tm, tn), lambda i, j, k: (i, j)))
        args.append(residual)
    out_bytes = jnp.dtype(out_dtype).itemsize
    block_bytes = (tm * tk * 2 + tk * tn * 2 + tm * tn * out_bytes
                   + (tm * tn * 4 if residual is not None else 0))
    scratch = [pltpu.VMEM((tm, tn), F32)] if nk > 1 else []
    return pl.pallas_call(
        functools.partial(_mm_kernel, n_a=len(a_list), k_sizes=blk_k, nk=nk,
                          has_res=residual is not None, relu2=relu2),
        grid=(m // tm, n // tn, nk),
        in_specs=in_specs,
        out_specs=pl.BlockSpec((tm, tn), lambda i, j, k: (i, j)),
        out_shape=jax.ShapeDtypeStruct((m, n), out_dtype),
        scratch_shapes=scratch,
        compiler_params=pltpu.CompilerParams(
            dimension_semantics=("parallel", "parallel", "arbitrary"),
            vmem_limit_bytes=_vmem_limit(block_bytes + tm * tn * 2)),
        name=name,
    )(*args)


def _lru_kernel(xb_ref, xh_ref, gb_ref, cw_ref, cb_ref, wa_ref, ba_ref, wx_ref, bx_ref, lam_ref,
                o_ref, xs_ref, a_s, u_s, h_s, hprev_ref, *, tt, cw):
    t = pl.program_id(2)

    @pl.when(t == 0)
    def _():
        hprev_ref[...] = jnp.zeros_like(hprev_ref)

    xs_ref[0:SUBLANES, :] = jnp.where(t == 0, 0.0, xh_ref[...])
    xs_ref[SUBLANES:SUBLANES + tt, :] = xb_ref[...]
    w = cw_ref[...]
    xc = cb_ref[...]
    for j in range(CONV_WIDTH):
        start = SUBLANES - (CONV_WIDTH - 1) + j
        xc = xc + xs_ref[start:start + tt, :] * w[j:j + 1, :]

    gate_a, gate_x = [], []
    for n in range(cw // LRU_BLOCK_DIM):
        xh = xc[:, n * LRU_BLOCK_DIM:(n + 1) * LRU_BLOCK_DIM].astype(BF16)
        gate_a.append(_dot(xh, wa_ref[n]))
        gate_x.append(_dot(xh, wx_ref[n]))
    r = jax.nn.sigmoid(jnp.concatenate(gate_a, axis=1) + ba_ref[...])
    i = jax.nn.sigmoid(jnp.concatenate(gate_x, axis=1) + bx_ref[...])
    log_a = (-LRU_C) * r * _softplus(-lam_ref[...])
    a_s[...] = jnp.exp(log_a)
    th = jnp.tanh(log_a)
    u_s[...] = jnp.sqrt(-2.0 * th / (1.0 - th)) * (i * xc)

    row = lax.broadcasted_iota(jnp.int32, (SUBLANES, cw), 0)

    def body(g, hp):
        r0 = pl.multiple_of(g * SUBLANES, SUBLANES)
        a8 = a_s[pl.ds(r0, SUBLANES), :]
        u8 = u_s[pl.ds(r0, SUBLANES), :]
        for s in (1, 2, 4):
            ash = jnp.where(row >= s, pltpu.roll(a8, s, 0), 1.0)
            ush = jnp.where(row >= s, pltpu.roll(u8, s, 0), 0.0)
            u8 = a8 * ush + u8
            a8 = a8 * ash
        h8 = a8 * hp + u8
        h_s[pl.ds(r0, SUBLANES), :] = h8
        return h8[SUBLANES - 1:SUBLANES, :]

    hp = lax.fori_loop(0, tt // SUBLANES, body, hprev_ref[0:1, :])
    hprev_ref[0:1, :] = hp
    o_ref[...] = (h_s[...] * jax.nn.gelu(gb_ref[...])).astype(o_ref.dtype)


def _lru_branch(u, conv_w, conv_b, w_a, b_a, w_x, b_x, lam, *, batch, tp):
    m = u.shape[0]
    width = GROUP_WIDTH
    cw = 512
    tt = _tile(tp, (640, 384, 256, 128))
    nt, nc = tp // tt, width // cw
    nb = cw // LRU_BLOCK_DIM

    def row_blk(b, j, t):
        return b * nt + t

    in_specs = [
        pl.BlockSpec((tt, cw), lambda b, j, t: (row_blk(b, j, t), j)),
        pl.BlockSpec((SUBLANES, cw),
                     lambda b, j, t: (jnp.maximum(row_blk(b, j, t) * (tt // SUBLANES) - 1, 0), j)),
        pl.BlockSpec((tt, cw), lambda b, j, t: (row_blk(b, j, t), nc + j)),
        pl.BlockSpec((CONV_WIDTH, cw), lambda b, j, t: (0, j)),
        pl.BlockSpec((1, cw), lambda b, j, t: (0, j)),
        pl.BlockSpec((nb, LRU_BLOCK_DIM, LRU_BLOCK_DIM), lambda b, j, t: (j, 0, 0)),
        pl.BlockSpec((1, cw), lambda b, j, t: (0, j)),
        pl.BlockSpec((nb, LRU_BLOCK_DIM, LRU_BLOCK_DIM), lambda b, j, t: (j, 0, 0)),
        pl.BlockSpec((1, cw), lambda b, j, t: (0, j)),
        pl.BlockSpec((1, cw), lambda b, j, t: (0, j)),
    ]
    return pl.pallas_call(
        functools.partial(_lru_kernel, tt=tt, cw=cw),
        grid=(batch, nc, nt),
        in_specs=in_specs,
        out_specs=pl.BlockSpec((tt, cw), lambda b, j, t: (row_blk(b, j, t), j)),
        out_shape=jax.ShapeDtypeStruct((m, width), BF16),
        scratch_shapes=[pltpu.VMEM((tt + SUBLANES, cw), F32), pltpu.VMEM((tt, cw), F32),
                        pltpu.VMEM((tt, cw), F32), pltpu.VMEM((tt, cw), F32),
                        pltpu.VMEM((SUBLANES, cw), F32)],
        compiler_params=pltpu.CompilerParams(
            dimension_semantics=("parallel", "parallel", "arbitrary"),
            vmem_limit_bytes=_vmem_limit(tt * cw * 4 * 12)),
        name="rglru",
    )(u, u, u, conv_w.astype(F32), conv_b.reshape(1, width).astype(F32),
      w_a.astype(BF16), b_a.reshape(1, width).astype(F32),
      w_x.astype(BF16), b_x.reshape(1, width).astype(F32), lam.reshape(1, width).astype(F32))


def _headnorm_kernel(*refs, rotary, q_scale, width):
    q_ref, k_ref, v_ref, qg_ref, kg_ref = refs[:5]
    pos = 5
    if rotary:
        c_ref, s1_ref, s2_ref = refs[pos:pos + 3]
        pos += 3
    qo_ref, ko_ref, vo_ref = refs[pos:pos + 3]

    def norm(x_ref, g_ref, o_ref, scale):
        x = x_ref[...]
        cols = []
        for g in range(width // LANES):
            xs = x[:, g * LANES:(g + 1) * LANES]
            y = xs * lax.rsqrt(jnp.mean(xs * xs, axis=-1, keepdims=True) + NORM_EPS) * g_ref[...]
            if rotary:
                y = (y * c_ref[...] + pltpu.roll(y, LANES - ROT_HALF, 1) * s1_ref[...]
                     + pltpu.roll(y, ROT_HALF, 1) * s2_ref[...])
            if scale != 1.0:
                y = y * scale
            cols.append(y.astype(o_ref.dtype))
        o_ref[...] = jnp.concatenate(cols, axis=1)

    norm(q_ref, qg_ref, qo_ref, q_scale)
    norm(k_ref, kg_ref, ko_ref, 1.0)
    vo_ref[...] = v_ref[...].astype(vo_ref.dtype)


def _headnorm(u, q_gain, k_gain, *, q_scale, tp, rot_tables=None):
    m = u.shape[0]
    width = 512
    ncb = GROUP_WIDTH // width
    tr = _tile(tp, (640, 384, 256, 128))
    nt = tp // tr
    rotary = rot_tables is not None
    in_specs = [
        pl.BlockSpec((tr, width), lambda i, j: (i, j)),
        pl.BlockSpec((tr, width), lambda i, j: (i, ncb + j)),
        pl.BlockSpec((tr, width), lambda i, j: (i, 2 * ncb + j)),
        pl.BlockSpec((1, LANES), lambda i, j: (0, 0)),
        pl.BlockSpec((1, LANES), lambda i, j: (0, 0)),
    ]
    args = [u, u, u, q_gain.reshape(1, LANES).astype(F32), k_gain.reshape(1, LANES).astype(F32)]
    if rotary:
        in_specs += [pl.BlockSpec((tr, LANES), lambda i, j: (i % nt, 0))] * 3
        args += list(rot_tables)
    out = jax.ShapeDtypeStruct((m, GROUP_WIDTH), BF16)
    return pl.pallas_call(
        functools.partial(_headnorm_kernel, rotary=rotary, q_scale=q_scale, width=width),
        grid=(m // tr, ncb),
        in_specs=in_specs,
        out_specs=[pl.BlockSpec((tr, width), lambda i, j: (i, j))] * 3,
        out_shape=[out, out, out],
        compiler_params=pltpu.CompilerParams(
            dimension_semantics=("parallel", "parallel"),
            vmem_limit_bytes=_vmem_limit(tr * width * 24)),
        name="headnorm_rot" if rotary else "headnorm",
    )(*args)


def _rotary_tables(tp):
    inv_freq = ROPE_THETA ** (-jnp.arange(ROT_HALF, dtype=F32) / ROT_HALF)
    ang = jnp.arange(tp, dtype=F32)[:, None] * inv_freq[None, :]
    cos, sin = jnp.cos(ang), jnp.sin(ang)
    ones = jnp.ones((tp, LANES - 2 * ROT_HALF), F32)
    zeros = jnp.zeros((tp, LANES - 2 * ROT_HALF), F32)
    zh = jnp.zeros((tp, ROT_HALF), F32)
    c = jnp.concatenate([cos, cos, ones], axis=1)
    s1 = jnp.concatenate([-sin, zh, zeros], axis=1)
    s2 = jnp.concatenate([zh, sin, zeros], axis=1)
    return c, s1, s2


def _causal_steps(nq):
    qi = np.concatenate([np.full(i + 1, i, np.int32) for i in range(nq)])
    ki = np.concatenate([np.arange(i + 1, dtype=np.int32) for i in range(nq)])
    return jnp.asarray(qi), jnp.asarray(ki)


def _softmax_step(s, v, m_ref, l_ref, acc_ref):
    m_prev = m_ref[...]
    m_new = jnp.maximum(m_prev, jnp.max(s, axis=1, keepdims=True))
    alpha = jnp.exp(m_prev - m_new)
    p = jnp.exp(s - m_new)
    l_ref[...] = alpha * l_ref[...] + jnp.sum(p, axis=1, keepdims=True)
    acc_ref[...] = alpha * acc_ref[...] + _dot(p.astype(BF16), v)
    m_ref[...] = m_new


def _init_softmax(m_ref, l_ref, acc_ref):
    m_ref[...] = jnp.full_like(m_ref, MASK_VALUE)
    l_ref[...] = jnp.zeros_like(l_ref)
    acc_ref[...] = jnp.zeros_like(acc_ref)


def _diff_flash_kernel(qt_ref, kt_ref, q1_ref, q2_ref, k1_ref, k2_ref, v_ref,
                       lq1_ref, lk1_ref, lq2_ref, lk2_ref, sg_ref, o_ref,
                       m1, l1, acc1, m2, l2, acc2, *, lambda_init, tq):
    s_idx = pl.program_id(2)
    qi = qt_ref[s_idx]
    ki = kt_ref[s_idx]

    @pl.when(ki == 0)
    def _():
        _init_softmax(m1, l1, acc1)
        _init_softmax(m2, l2, acc2)

    def step(masked):
        v = v_ref[...]
        for q_ref, k_ref, m, l, acc in ((q1_ref, k1_ref, m1, l1, acc1), (q2_ref, k2_ref, m2, l2, acc2)):
            s = _nt(q_ref[...], k_ref[...])
            if masked:
                row = lax.broadcasted_iota(jnp.int32, s.shape, 0)
                col = lax.broadcasted_iota(jnp.int32, s.shape, 1)
                s = jnp.where(col <= row, s, MASK_VALUE)
            _softmax_step(s, v, m, l, acc)

    @pl.when(ki < qi)
    def _():
        step(False)

    @pl.when(ki == qi)
    def _():
        step(True)
        lam = (jnp.exp(jnp.sum(lq1_ref[...] * lk1_ref[...], axis=1, keepdims=True))
               - jnp.exp(jnp.sum(lq2_ref[...] * lk2_ref[...], axis=1, keepdims=True)) + lambda_init)
        o = acc1[...] / l1[...] - lam * (acc2[...] / l2[...])
        o = o * lax.rsqrt(jnp.mean(o * o, axis=-1, keepdims=True) + NORM_EPS)
        o_ref[...] = (o * sg_ref[...] * (1.0 - lambda_init)).astype(o_ref.dtype)


def _diff_attention(qn, kn, vb, lq1, lk1, lq2, lk2, sub_gain, *, lambda_init, batch, tp):
    m = qn.shape[0]
    tq = _tile(tp, (640, 384, 256, 128))
    nq = tp // tq
    qt, kt = _causal_steps(nq)
    hd, vd = DIFF_HEAD_DIM, 2 * DIFF_HEAD_DIM
    vec = lambda p: p.reshape(1, -1).astype(F32)
    grid_spec = pltpu.PrefetchScalarGridSpec(
        num_scalar_prefetch=2,
        grid=(batch, DIFF_HEADS, int(qt.shape[0])),
        in_specs=[
            pl.BlockSpec((tq, hd), lambda b, h, s, qt, kt: (b * nq + qt[s], 2 * h)),
            pl.BlockSpec((tq, hd), lambda b, h, s, qt, kt: (b * nq + qt[s], 2 * h + 1)),
            pl.BlockSpec((tq, hd), lambda b, h, s, qt, kt: (b * nq + kt[s], 2 * h)),
            pl.BlockSpec((tq, hd), lambda b, h, s, qt, kt: (b * nq + kt[s], 2 * h + 1)),
            pl.BlockSpec((tq, vd), lambda b, h, s, qt, kt: (b * nq + kt[s], h)),
            pl.BlockSpec((1, hd), lambda b, h, s, qt, kt: (0, 0)),
            pl.BlockSpec((1, hd), lambda b, h, s, qt, kt: (0, 0)),
            pl.BlockSpec((1, hd), lambda b, h, s, qt, kt: (0, 0)),
            pl.BlockSpec((1, hd), lambda b, h, s, qt, kt: (0, 0)),
            pl.BlockSpec((1, vd), lambda b, h, s, qt, kt: (0, 0)),
        ],
        out_specs=pl.BlockSpec((tq, vd), lambda b, h, s, qt, kt: (b * nq + qt[s], h)),
        scratch_shapes=[pltpu.VMEM((tq, 1), F32), pltpu.VMEM((tq, 1), F32), pltpu.VMEM((tq, vd), F32),
                        pltpu.VMEM((tq, 1), F32), pltpu.VMEM((tq, 1), F32), pltpu.VMEM((tq, vd), F32)],
    )
    return pl.pallas_call(
        functools.partial(_diff_flash_kernel, lambda_init=lambda_init, tq=tq),
        grid_spec=grid_spec,
        out_shape=jax.ShapeDtypeStruct((m, GROUP_WIDTH), BF16),
        compiler_params=pltpu.CompilerParams(
            dimension_semantics=("parallel", "parallel", "arbitrary"),
            vmem_limit_bytes=_vmem_limit(tq * tq * 4 * 8 + tq * 2048 * 4)),
        name="diff_attention",
    )(qt, kt, qn, qn, kn, kn, vb, vec(lq1), vec(lk1), vec(lq2), vec(lk2), vec(sub_gain))


def _fox_flash_kernel(qt_ref, kt_ref, q_ref, k_ref, v_ref, ck_ref, cq_ref, o_ref, m, l, acc):
    s_idx = pl.program_id(2)
    qi = qt_ref[s_idx]
    ki = kt_ref[s_idx]

    @pl.when(ki == 0)
    def _():
        _init_softmax(m, l, acc)

    def step(masked):
        bias = cq_ref[0][:, 0:1] - ck_ref[0]
        s = _nt(q_ref[...], k_ref[...]) + bias
        if masked:
            row = lax.broadcasted_iota(jnp.int32, s.shape, 0)
            col = lax.broadcasted_iota(jnp.int32, s.shape, 1)
            s = jnp.where(col <= row, s, MASK_VALUE)
        _softmax_step(s, v_ref[...], m, l, acc)

    @pl.when(ki < qi)
    def _():
        step(False)

    @pl.when(ki == qi)
    def _():
        step(True)
        o_ref[...] = (acc[...] / l[...]).astype(o_ref.dtype)


def _fox_attention(qn, kn, vb, c_rows, *, batch, tp):
    m = qn.shape[0]
    tq = _tile(tp, (640, 384, 256, 128))
    nq = tp // tq
    qt, kt = _causal_steps(nq)
    hd = FOX_HEAD_DIM
    lane_blocks = tq // LANES
    grid_spec = pltpu.PrefetchScalarGridSpec(
        num_scalar_prefetch=2,
        grid=(batch, FOX_HEADS, int(qt.shape[0])),
        in_specs=[
            pl.BlockSpec((tq, hd), lambda b, h, s, qt, kt: (b * nq + qt[s], h)),
            pl.BlockSpec((tq, hd), lambda b, h, s, qt, kt: (b * nq + kt[s], h)),
            pl.BlockSpec((tq, hd), lambda b, h, s, qt, kt: (b * nq + kt[s], h)),
            pl.BlockSpec((1, 1, tq), lambda b, h, s, qt, kt: (b * FOX_HEADS + h, 0, kt[s])),
            pl.BlockSpec((1, 1, LANES), lambda b, h, s, qt, kt: (b * FOX_HEADS + h, 0, qt[s] * lane_blocks)),
        ],
        out_specs=pl.BlockSpec((tq, hd), lambda b, h, s, qt, kt: (b * nq + qt[s], h)),
        scratch_shapes=[pltpu.VMEM((tq, 1), F32), pltpu.VMEM((tq, 1), F32), pltpu.VMEM((tq, hd), F32)],
    )
    return pl.pallas_call(
        _fox_flash_kernel,
        grid_spec=grid_spec,
        out_shape=jax.ShapeDtypeStruct((m, GROUP_WIDTH), BF16),
        compiler_params=pltpu.CompilerParams(
            dimension_semantics=("parallel", "parallel", "arbitrary"),
            vmem_limit_bytes=_vmem_limit(tq * tq * 4 * 6 + tq * 2048 * 2)),
        name="fox_attention",
    )(qt, kt, qn, kn, vb, c_rows, c_rows)


def _fox_gate_kernel(hn_ref, wf_ref, bias_ref, c_ref, carry_ref, *, tc):
    t = pl.program_id(1)

    @pl.when(t == 0)
    def _():
        carry_ref[...] = jnp.zeros_like(carry_ref)

    f = _nt(wf_ref[...], hn_ref[...]) + bias_ref[...]
    log_f = jnp.minimum(f, 0.0) - jnp.log1p(jnp.exp(-jnp.abs(f)))
    upper = (lax.broadcasted_iota(jnp.int32, (tc, tc), 0)
             <= lax.broadcasted_iota(jnp.int32, (tc, tc), 1)).astype(BF16)
    hi, mid, lo = _split3(log_f)
    c = _dot(hi, upper) + _dot(mid, upper) + _dot(lo, upper) + carry_ref[:, 0:1]
    c_ref[0] = c
    carry_ref[...] = jnp.broadcast_to(c[:, tc - 1:tc], carry_ref.shape)


def _fox_gates(hn, wf_t, f_bias, *, batch, tp):
    d = hn.shape[1]
    tc = _tile(tp, (640, 384, 256, 128))
    nt = tp // tc
    return pl.pallas_call(
        functools.partial(_fox_gate_kernel, tc=tc),
        grid=(batch, nt),
        in_specs=[pl.BlockSpec((tc, d), lambda b, t: (b * nt + t, 0)),
                  pl.BlockSpec((FOX_HEADS, d), lambda b, t: (0, 0)),
                  pl.BlockSpec((FOX_HEADS, 1), lambda b, t: (0, 0))],
        out_specs=pl.BlockSpec((1, FOX_HEADS, tc), lambda b, t: (b, 0, t)),
        out_shape=jax.ShapeDtypeStruct((batch, FOX_HEADS, tp), F32),
        scratch_shapes=[pltpu.VMEM((FOX_HEADS, LANES), F32)],
        compiler_params=pltpu.CompilerParams(
            dimension_semantics=("parallel", "arbitrary"),
            vmem_limit_bytes=_vmem_limit(tc * d * 2 + tc * tc * 8)),
        name="fox_gates",
    )(hn, wf_t, f_bias.reshape(FOX_HEADS, 1).astype(F32))


def _head_sum(x, ones_bd):
    cols = []
    for g in range(x.shape[1] // LANES):
        hi, lo = _split2(x[:, g * LANES:(g + 1) * LANES])
        cols.append(_dot(hi, ones_bd) + _dot(lo, ones_bd))
    return jnp.concatenate(cols, axis=1)


def _head_ones():
    r = lax.broadcasted_iota(jnp.int32, (LANES, LANES), 0) < RWKV_HEAD_DIM
    c = lax.broadcasted_iota(jnp.int32, (LANES, LANES), 1) < RWKV_HEAD_DIM
    return jnp.where(r == c, 1.0, 0.0).astype(BF16)


def _rwkv_pre_kernel(zr_ref, zrh_ref, zk_ref, zkh_ref, zv_ref, zvh_ref, zl_ref, zlh_ref,
                     mur_ref, muk_ref, muv_ref, mul_ref, w0_ref, wup_ref, a0_ref, aup_ref, gup_ref,
                     kk_ref, ka_ref,
                     r_o, k_o, v_o, a_o, b_o, lw_o, g_o, xs_ref, ls_ref, *, tr, nt):
    first = (pl.program_id(0) % nt) == 0

    def shift(x_ref, h_ref, mu_ref, s_ref):
        s_ref[0:SUBLANES, :] = jnp.where(first, 0.0, h_ref[...])
        s_ref[SUBLANES:SUBLANES + tr, :] = x_ref[...]
        x = x_ref[...]
        return x + (s_ref[SUBLANES - 1:SUBLANES - 1 + tr, :] - x) * mu_ref[...]

    zl = shift(zl_ref, zlh_ref, mul_ref, ls_ref)
    wd = zl[:, 0:DECAY_LORA]
    ad = zl[:, DECAY_LORA:DECAY_LORA + AAA_LORA]
    gd = zl[:, DECAY_LORA + AAA_LORA:]
    r = shift(zr_ref, zrh_ref, mur_ref, xs_ref)
    k = shift(zk_ref, zkh_ref, muk_ref, xs_ref)
    v = shift(zv_ref, zvh_ref, muv_ref, xs_ref)

    w_log = -_softplus(-(w0_ref[...] + _dot(jnp.tanh(wd).astype(BF16), wup_ref[...]))) - 0.5
    a = jax.nn.sigmoid(a0_ref[...] + _dot(ad.astype(BF16), aup_ref[...]))
    g = _dot(jax.nn.sigmoid(gd).astype(BF16), gup_ref[...])
    kx = k * kk_ref[...]
    kk = kx / jnp.maximum(jnp.sqrt(_head_sum(kx * kx, _head_ones())), 1e-12)
    r_o[...] = r
    k_o[...] = k * (1.0 + (a - 1.0) * ka_ref[...])
    v_o[...] = v
    a_o[...] = -kk
    b_o[...] = kk * a
    lw_o[...] = -jnp.exp(w_log)
    g_o[...] = g


def _rwkv_pre(z, mu, w0, w_up, a0, a_up, g_up, k_k, k_a, *, tp):
    m = z.shape[0]
    width = GROUP_WIDTH
    cw = 512
    ncb = width // cw
    lw = RWKV_LORA_PAD
    tr = _tile(tp, (320, 384, 256, 128))
    nt = tp // tr
    hb = tr // SUBLANES

    def halo(i):
        return jnp.maximum(i * hb - 1, 0)

    mu2 = mu.reshape(1, -1).astype(F32)
    row = lambda p: p.reshape(1, width).astype(F32)
    g_up_p = jnp.zeros((lw - DECAY_LORA - AAA_LORA, width), BF16).at[:GATE_LORA].set(g_up.astype(BF16))
    in_specs = []
    for c in range(3):
        in_specs.append(pl.BlockSpec((tr, cw), lambda i, j, c=c: (i, c * ncb + j)))
        in_specs.append(pl.BlockSpec((SUBLANES, cw), lambda i, j, c=c: (halo(i), c * ncb + j)))
    in_specs.append(pl.BlockSpec((tr, lw), lambda i, j: (i, 3 * width // lw)))
    in_specs.append(pl.BlockSpec((SUBLANES, lw), lambda i, j: (halo(i), 3 * width // lw)))
    for c in range(3):
        in_specs.append(pl.BlockSpec((1, cw), lambda i, j, c=c: (0, c * ncb + j)))
    in_specs.append(pl.BlockSpec((1, lw), lambda i, j: (0, 3 * width // lw)))
    in_specs += [
        pl.BlockSpec((1, cw), lambda i, j: (0, j)),
        pl.BlockSpec((DECAY_LORA, cw), lambda i, j: (0, j)),
        pl.BlockSpec((1, cw), lambda i, j: (0, j)),
        pl.BlockSpec((AAA_LORA, cw), lambda i, j: (0, j)),
        pl.BlockSpec((lw - DECAY_LORA - AAA_LORA, cw), lambda i, j: (0, j)),
        pl.BlockSpec((1, cw), lambda i, j: (0, j)),
        pl.BlockSpec((1, cw), lambda i, j: (0, j)),
    ]
    out = jax.ShapeDtypeStruct((m, width), F32)
    return pl.pallas_call(
        functools.partial(_rwkv_pre_kernel, tr=tr, nt=nt),
        grid=(m // tr, ncb),
        in_specs=in_specs,
        out_specs=[pl.BlockSpec((tr, cw), lambda i, j: (i, j))] * 7,
        out_shape=[out] * 7,
        scratch_shapes=[pltpu.VMEM((tr + SUBLANES, cw), F32), pltpu.VMEM((tr + SUBLANES, lw), F32)],
        compiler_params=pltpu.CompilerParams(
            dimension_semantics=("parallel", "parallel"),
            vmem_limit_bytes=_vmem_limit(tr * cw * 4 * 16 + tr * lw * 4 * 3)),
        name="rwkv_pre",
    )(z, z, z, z, z, z, z, z, mu2, mu2, mu2, mu2, row(w0), w_up.astype(BF16), row(a0),
      a_up.astype(BF16), g_up_p, row(k_k), row(k_a))


def _rwkv_scan_kernel(r_ref, k_ref, v_ref, a_ref, b_ref, lw_ref, y_ref, s_ref, *, tt, pairs):
    C = RWKV_CHUNK
    t = pl.program_id(2)

    @pl.when(t == 0)
    def _():
        s_ref[...] = jnp.zeros_like(s_ref)

    lane_top = lax.broadcasted_iota(jnp.int32, (C, LANES), 1) < RWKV_HEAD_DIM
    ri = lax.broadcasted_iota(jnp.int32, (2 * C, 2 * C), 0)
    ci = lax.broadcasted_iota(jnp.int32, (2 * C, 2 * C), 1)
    strict = (ci & (C - 1)) < (ri & (C - 1))
    incl = (ci & (C - 1)) <= (ri & (C - 1))
    eye = (ri == ci).astype(F32)
    lower = (lax.broadcasted_iota(jnp.int32, (C, C), 1)
             <= lax.broadcasted_iota(jnp.int32, (C, C), 0)).astype(BF16)

    def stack(x):
        return jnp.concatenate([jnp.where(lane_top, x, 0.0), jnp.where(lane_top, 0.0, x)], axis=0)

    def chunk(c, carry):
        r0 = pl.multiple_of(c * C, C)
        for p in range(pairs):
            sl = slice(p * LANES, (p + 1) * LANES)
            lw = lw_ref[pl.ds(r0, C), sl]
            hi, mid, lo = _split3(lw)
            g_in = _dot(lower, hi) + _dot(lower, mid) + _dot(lower, lo)
            g_ex = g_in - lw
            g_last = g_in[C - 1:C, :]
            e_in, e_ex, e_neg, e_rem = jnp.exp(g_in), jnp.exp(g_ex), jnp.exp(-g_in), jnp.exp(g_last - g_in)
            r = r_ref[pl.ds(r0, C), sl]
            k = k_ref[pl.ds(r0, C), sl]
            a = a_ref[pl.ds(r0, C), sl]
            b = b_ref[pl.ds(r0, C), sl]
            xa = stack(a * e_ex).astype(BF16)
            xr = stack(r * e_in).astype(BF16)
            yb = stack(b * e_neg).astype(BF16)
            yk = stack(k * e_neg).astype(BF16)
            vs = stack(v_ref[pl.ds(r0, C), sl]).astype(BF16)
            bk = jnp.concatenate([stack(b * e_rem), stack(k * e_rem)], axis=0).astype(BF16)

            l_ab = jnp.where(strict, _nt(xa, yb), 0.0)
            l_ak = jnp.where(strict, _nt(xa, yk), 0.0)
            m_rb = jnp.where(incl, _nt(xr, yb), 0.0)
            m_rk = jnp.where(incl, _nt(xr, yk), 0.0)

            x = l_ab
            tinv = eye + x
            for _ in range(int(math.log2(C)) - 1):
                x = _dot3(x, x)
                tinv = tinv + _dot3(tinv, x)

            s0 = s_ref[p]
            s0b = s0.astype(BF16)
            w = _nt(xa, s0b) + _dot(l_ak.astype(BF16), vs)
            u = _dot3(tinv, w)
            ub = u.astype(BF16)
            ys = _nt(xr, s0b) + _dot(m_rb.astype(BF16), ub) + _dot(m_rk.astype(BF16), vs)
            y_ref[pl.ds(r0, C), sl] = ys[0:C, :] + ys[C:2 * C, :]
            uv = jnp.concatenate([ub, vs], axis=0)
            s_ref[p] = s0 * jnp.exp(g_last) + _tn(uv, bk)
        return carry

    lax.fori_loop(0, tt // C, chunk, 0)


def _rwkv_scan(r, k, v, a, b, lw, *, batch, tp):
    m = r.shape[0]
    pairs = 4
    cw = pairs * LANES
    tt = _tile(tp, (640, 384, 256, 128))
    nt, nc = tp // tt, GROUP_WIDTH // cw
    spec = pl.BlockSpec((tt, cw), lambda bi, j, t: (bi * nt + t, j))
    return pl.pallas_call(
        functools.partial(_rwkv_scan_kernel, tt=tt, pairs=pairs),
        grid=(batch, nc, nt),
        in_specs=[spec] * 6,
        out_specs=spec,
        out_shape=jax.ShapeDtypeStruct((m, GROUP_WIDTH), F32),
        scratch_shapes=[pltpu.VMEM((pairs, LANES, LANES), F32)],
        compiler_params=pltpu.CompilerParams(
            dimension_semantics=("parallel", "parallel", "arbitrary"),
            vmem_limit_bytes=_vmem_limit(tt * cw * 4 * 8)),
        name="rwkv_scan",
    )(r, k, v, a, b, lw)


def _rwkv_post_kernel(y_ref, r_ref, k_ref, v_ref, g_ref, rk_ref, gw_ref, gb_ref, o_ref):
    ones_bd = _head_ones()
    inv_n = 1.0 / RWKV_HEAD_DIM
    y = y_ref[...]
    mean = _head_sum(y, ones_bd) * inv_n
    d = y - mean
    var = _head_sum(d * d, ones_bd) * inv_n
    yn = d * lax.rsqrt(var + RWKV_GN_EPS) * gw_ref[...] + gb_ref[...]
    bonus = _head_sum(r_ref[...] * k_ref[...] * rk_ref[...], ones_bd) * v_ref[...]
    o_ref[...] = ((yn + bonus) * g_ref[...]).astype(o_ref.dtype)


def _rwkv_post(y, r, k, v, g, r_k, gn_w, gn_b, *, tp):
    m = y.shape[0]
    cw = 512
    tr = _tile(tp, (640, 384, 256, 128))
    spec = pl.BlockSpec((tr, cw), lambda i, j: (i, j))
    pspec = pl.BlockSpec((1, cw), lambda i, j: (0, j))
    row = lambda p: p.reshape(1, GROUP_WIDTH).astype(F32)
    return pl.pallas_call(
        _rwkv_post_kernel,
        grid=(m // tr, GROUP_WIDTH // cw),
        in_specs=[spec] * 5 + [pspec] * 3,
        out_specs=spec,
        out_shape=jax.ShapeDtypeStruct((m, GROUP_WIDTH), BF16),
        compiler_params=pltpu.CompilerParams(
            dimension_semantics=("parallel", "parallel"),
            vmem_limit_bytes=_vmem_limit(tr * cw * 4 * 10)),
        name="rwkv_post",
    )(y, r, k, v, g, row(r_k), row(gn_w), row(gn_b))


def _mlp(h, f_gain, f_up, f_down, tm):
    hn = _rmsnorm_rows(h, f_gain)
    hid = _matmul([hn], f_up.astype(BF16), n_off=0, n=D_FF, tm=tm, tn=512, out_dtype=BF16,
                  relu2=True, name="ffn_up")
    return _matmul([hid], f_down.astype(BF16), n_off=0, n=D_MODEL, tm=tm, tn=1024, tk=2048,
                   residual=h, name="ffn_down")


def kernel(x, meta_tokens, mix_norm_0, w_in_0, conv_w_0, conv_b_0, lru_wa_0, lru_ba_0, lru_wx_0, lru_bx_0, lru_lam_0, diff_q_gain_0, diff_k_gain_0, diff_lq1_0, diff_lk1_0, diff_lq2_0, diff_lk2_0, diff_sub_gain_0, w_out_0, ffn_norm_0, ffn_up_0, ffn_down_0, mix_norm_1, w_in_1, rwkv_mu_1, rwkv_w0_1, rwkv_w_up_1, rwkv_a0_1, rwkv_a_up_1, rwkv_g_up_1, rwkv_k_k_1, rwkv_k_a_1, rwkv_r_k_1, rwkv_gn_w_1, rwkv_gn_b_1, fox_q_gain_1, fox_k_gain_1, fox_f_bias_1, w_out_1, ffn_norm_1, ffn_up_1, ffn_down_1):
    batch, seq, d = x.shape
    assert d == D_MODEL
    t_real = seq + N_META
    tp = -(-t_real // LANES) * LANES
    m = batch * tp
    tm = _tile(m, (1280, 768, 640, 512, 384, 256, 128))
    gw = GROUP_WIDTH

    meta = jnp.broadcast_to(meta_tokens[None].astype(x.dtype), (batch, N_META, d))
    pad = jnp.zeros((batch, tp - t_real, d), x.dtype)
    h = jnp.concatenate([meta, x, pad], axis=1).reshape(m, d)

    hn = _rmsnorm_rows(h, mix_norm_0)
    w_in = w_in_0.astype(BF16)
    u_lru = _matmul([hn], w_in, n_off=0, n=2 * gw, tm=tm, tn=512, name="in_proj0_lru")
    u_att = _matmul([hn], w_in, n_off=2 * gw, n=3 * gw, tm=tm, tn=512, name="in_proj0_att")
    y_a = _lru_branch(u_lru, conv_w_0, conv_b_0, lru_wa_0, lru_ba_0, lru_wx_0, lru_bx_0, lru_lam_0,
                      batch=batch, tp=tp)
    qn, kn, vb = _headnorm(u_att, diff_q_gain_0, diff_k_gain_0, q_scale=DIFF_HEAD_DIM ** -0.5, tp=tp,
                           rot_tables=_rotary_tables(tp))
    lambda_init = 0.8 - 0.6 * math.exp(-0.3 * 0)
    y_b = _diff_attention(qn, kn, vb, diff_lq1_0, diff_lk1_0, diff_lq2_0, diff_lk2_0, diff_sub_gain_0,
                          lambda_init=lambda_init, batch=batch, tp=tp)
    h = _matmul([y_a, y_b], w_out_0.astype(BF16), n_off=0, n=d, tm=tm, tn=512, residual=h,
                name="out_proj0")
    h = _mlp(h, ffn_norm_0, ffn_up_0, ffn_down_0, tm)

    hn = _rmsnorm_rows(h, mix_norm_1)
    slab = 3 * gw + DECAY_LORA + AAA_LORA + GATE_LORA
    zw = 3 * gw + RWKV_LORA_PAD
    w_z = jnp.zeros((d, zw), BF16).at[:, :slab].set(w_in_1[:, :slab].astype(BF16))
    w_f = w_in_1[:, slab:slab + 3 * gw].astype(BF16)
    wf_t = w_in_1[:, slab + 3 * gw:].T.astype(BF16)
    mu = jnp.zeros((zw,), F32).at[:slab].set(rwkv_mu_1.astype(F32))
    z = _matmul([hn], w_z, n_off=0, n=zw, tm=tm, tn=512, name="in_proj1_rwkv")
    u_fox = _matmul([hn], w_f, n_off=0, n=3 * gw, tm=tm, tn=512, name="in_proj1_fox")
    c_rows = _fox_gates(hn, wf_t, fox_f_bias_1, batch=batch, tp=tp)

    r, k32, v, a_neg, b_vec, log_w, g = _rwkv_pre(z, mu, rwkv_w0_1, rwkv_w_up_1, rwkv_a0_1, rwkv_a_up_1,
                                                  rwkv_g_up_1, rwkv_k_k_1, rwkv_k_a_1, tp=tp)
    y = _rwkv_scan(r, k32, v, a_neg, b_vec, log_w, batch=batch, tp=tp)
    y_c = _rwkv_post(y, r, k32, v, g, rwkv_r_k_1, rwkv_gn_w_1, rwkv_gn_b_1, tp=tp)

    fq, fk, fv = _headnorm(u_fox, fox_q_gain_1, fox_k_gain_1, q_scale=FOX_HEAD_DIM ** -0.5, tp=tp)
    y_d = _fox_attention(fq, fk, fv, c_rows.reshape(batch * FOX_HEADS, 1, tp), batch=batch, tp=tp)
    h = _matmul([y_c, y_d], w_out_1.astype(BF16), n_off=0, n=d, tm=tm, tn=512, residual=h,
                name="out_proj1")
    h = _mlp(h, ffn_norm_1, ffn_up_1, ffn_down_1, tm)

    return h.reshape(batch, tp, d)[:, N_META:N_META + seq]
```

```python
import functools
import math

import jax
import jax.numpy as jnp
from jax import lax
from jax.experimental import pallas as pl
from jax.experimental.pallas import tpu as pltpu

F32 = jnp.float32
BF16 = jnp.bfloat16

D_MODEL = 4096
N_META = 16
GROUP_WIDTH = D_MODEL // 2
D_FF = 4 * D_MODEL
NORM_EPS = 1e-6
ROPE_THETA = 500000.0

LRU_BLOCK_DIM = 128
CONV_WIDTH = 4
LRU_C = 8.0

DIFF_HEAD_DIM = 128
DIFF_HEADS = GROUP_WIDTH // (2 * DIFF_HEAD_DIM)
ROT_HALF = DIFF_HEAD_DIM // 8

RWKV_HEAD_DIM = 64
RWKV_CHUNK = 64
DECAY_LORA = 128
AAA_LORA = 128
GATE_LORA = 480
RWKV_LORA_PAD = 1024
RWKV_GN_EPS = 64e-5

FOX_HEAD_DIM = 128
FOX_HEADS = GROUP_WIDTH // FOX_HEAD_DIM

LANES = 128
SUBLANES = 8
VMEM_CAP_BYTES = 56 * 1024 * 1024
MASK_VALUE = -1e30
LOG2E = math.log2(math.e)
KV_GROUP = 2


def _tile(n, prefs):
    for p in prefs:
        if n % p == 0:
            return p
    raise ValueError(f"no tile of {prefs} divides {n}")


def _vmem_limit(block_bytes):
    return int(min(VMEM_CAP_BYTES, 2 * block_bytes + (8 << 20)))


def _nt(a, b):
    return lax.dot_general(a, b, (((1,), (1,)), ((), ())), preferred_element_type=F32)


def _tn(a, b):
    return lax.dot_general(a, b, (((0,), (0,)), ((), ())), preferred_element_type=F32)


def _dot(a, b):
    return jnp.dot(a, b, preferred_element_type=F32)


def _split2(x):
    hi = x.astype(BF16)
    lo = (x - hi.astype(F32)).astype(BF16)
    return hi, lo


def _split3(x):
    hi = x.astype(BF16)
    r1 = x - hi.astype(F32)
    mid = r1.astype(BF16)
    lo = (r1 - mid.astype(F32)).astype(BF16)
    return hi, mid, lo


def _dot3(a, b):
    ah, al = _split2(a)
    bh, bl = _split2(b)
    return _dot(ah, bh) + _dot(ah, bl) + _dot(al, bh)


def _softplus(x):
    return jnp.maximum(x, 0.0) + jnp.log1p(jnp.exp(-jnp.abs(x)))


def _rmsnorm_kernel(x_ref, g_ref, o_ref):
    x = x_ref[...]
    y = x * lax.rsqrt(jnp.mean(x * x, axis=-1, keepdims=True) + NORM_EPS)
    o_ref[...] = (y * g_ref[...]).astype(o_ref.dtype)


def _rmsnorm_rows(x, gain):
    m, d = x.shape
    tr = _tile(m, (320, 256, 128))
    return pl.pallas_call(
        _rmsnorm_kernel,
        grid=(m // tr,),
        in_specs=[pl.BlockSpec((tr, d), lambda i: (i, 0)),
                  pl.BlockSpec((1, d), lambda i: (0, 0))],
        out_specs=pl.BlockSpec((tr, d), lambda i: (i, 0)),
        out_shape=jax.ShapeDtypeStruct((m, d), BF16),
        compiler_params=pltpu.CompilerParams(
            dimension_semantics=("parallel",),
            vmem_limit_bytes=_vmem_limit(tr * d * 6)),
        name="rmsnorm_rows",
    )(x, gain.reshape(1, d).astype(F32))


def _mm_kernel(*refs, n_a, k_sizes, nk, has_res, relu2):
    a_refs = refs[:n_a]
    w_ref = refs[n_a]
    pos = n_a + 1
    res_ref = None
    if has_res:
        res_ref = refs[pos]
        pos += 1
    o_ref = refs[pos]
    acc_ref = refs[pos + 1] if nk > 1 else None

    def partial_product():
        acc = None
        off = 0
        for a_ref, ks in zip(a_refs, k_sizes):
            p = _dot(a_ref[...], w_ref[off:off + ks, :])
            acc = p if acc is None else acc + p
            off += ks
        return acc

    def finish(acc):
        if relu2:
            acc = jnp.square(jnp.maximum(acc, 0.0))
        if has_res:
            acc = acc + res_ref[...]
        o_ref[...] = acc.astype(o_ref.dtype)

    if nk == 1:
        finish(partial_product())
        return

    k = pl.program_id(2)

    @pl.when(k == 0)
    def _():
        acc_ref[...] = jnp.zeros_like(acc_ref)

    acc_ref[...] += partial_product()

    @pl.when(k == nk - 1)
    def _():
        finish(acc_ref[...])


def _matmul(a_list, w, *, n_off, n, tm, tn, tk=None, out_dtype=F32, residual=None, relu2=False, name):
    m = a_list[0].shape[0]
    k_sizes = tuple(a.shape[1] for a in a_list)
    k_total = sum(k_sizes)
    if tk is None:
        tk = k_total
    assert len(a_list) == 1 or tk == k_total
    assert m % tm == 0 and n % tn == 0 and n_off % tn == 0 and k_total % tk == 0
    nk = k_total // tk
    j_off = n_off // tn
    blk_k = (tk,) if len(a_list) == 1 else k_sizes
    in_specs = [pl.BlockSpec((tm, bk), lambda i, j, k: (i, k)) for bk in blk_k]
    in_specs.append(pl.BlockSpec((tk, tn), lambda i, j, k: (k, j + j_off)))
    args = list(a_list) + [w]
    if residual is not None:
        in_specs.append(pl.BlockSpec((tm, tn), lambda i, j, k: (i, j)))
        args.append(residual)
    out_bytes = jnp.dtype(out_dtype).itemsize
    block_bytes = (tm * tk * 2 + tk * tn * 2 + tm * tn * out_bytes
                   + (tm * tn * 4 if residual is not None else 0))
    scratch = [pltpu.VMEM((tm, tn), F32)] if nk > 1 else []
    return pl.pallas_call(
        functools.partial(_mm_kernel, n_a=len(a_list), k_sizes=blk_k, nk=nk,
                          has_res=residual is not None, relu2=relu2),
        grid=(m // tm, n // tn, nk),
        in_specs=in_specs,
        out_specs=pl.BlockSpec((tm, tn), lambda i, j, k: (i, j)),
        out_shape=jax.ShapeDtypeStruct((m, n), out_dtype),
        scratch_shapes=scratch,
        compiler_params=pltpu.CompilerParams(
            dimension_semantics=("parallel", "parallel", "arbitrary"),
            vmem_limit_bytes=_vmem_limit(block_bytes + tm * tn * 2)),
        name=name,
    )(*args)


def _lru_kernel(xb_ref, xh_ref, gb_ref, cw_ref, cb_ref, wa_ref, ba_ref, wx_ref, bx_ref, lam_ref,
                o_ref, xs_ref, a_s, u_s, h_s, hprev_ref, *, tt, cw):
    t = pl.program_id(2)

    @pl.when(t == 0)
    def _():
        hprev_ref[...] = jnp.zeros_like(hprev_ref)

    xs_ref[0:SUBLANES, :] = jnp.where(t == 0, 0.0, xh_ref[...])
    xs_ref[SUBLANES:SUBLANES + tt, :] = xb_ref[...]
    w = cw_ref[...]
    xc = cb_ref[...]
    for j in range(CONV_WIDTH):
        start = SUBLANES - (CONV_WIDTH - 1) + j
        xc = xc + xs_ref[start:start + tt, :] * w[j:j + 1, :]

    gate_a, gate_x = [], []
    for n in range(cw // LRU_BLOCK_DIM):
        xh = xc[:, n * LRU_BLOCK_DIM:(n + 1) * LRU_BLOCK_DIM].astype(BF16)
        gate_a.append(_dot(xh, wa_ref[n]))
        gate_x.append(_dot(xh, wx_ref[n]))
    r = jax.nn.sigmoid(jnp.concatenate(gate_a, axis=1) + ba_ref[...])
    i = jax.nn.sigmoid(jnp.concatenate(gate_x, axis=1) + bx_ref[...])
    log_a = (-LRU_C) * r * _softplus(-lam_ref[...])
    a_s[...] = jnp.exp(log_a)
    th = jnp.tanh(log_a)
    u_s[...] = jnp.sqrt(-2.0 * th / (1.0 - th)) * (i * xc)

    row = lax.broadcasted_iota(jnp.int32, (SUBLANES, cw), 0)

    def body(g, hp):
        r0 = pl.multiple_of(g * SUBLANES, SUBLANES)
        a8 = a_s[pl.ds(r0, SUBLANES), :]
        u8 = u_s[pl.ds(r0, SUBLANES), :]
        for s in (1, 2, 4):
            ash = jnp.where(row >= s, pltpu.roll(a8, s, 0), 1.0)
            ush = jnp.where(row >= s, pltpu.roll(u8, s, 0), 0.0)
            u8 = a8 * ush + u8
            a8 = a8 * ash
        h8 = a8 * hp + u8
        h_s[pl.ds(r0, SUBLANES), :] = h8
        return h8[SUBLANES - 1:SUBLANES, :]

    hp = lax.fori_loop(0, tt // SUBLANES, body, hprev_ref[0:1, :])
    hprev_ref[0:1, :] = hp
    o_ref[...] = (h_s[...] * jax.nn.gelu(gb_ref[...])).astype(o_ref.dtype)


def _lru_branch(u, conv_w, conv_b, w_a, b_a, w_x, b_x, lam, *, batch, tp):
    m = u.shape[0]
    width = GROUP_WIDTH
    cw = 512
    tt = _tile(tp, (640, 384, 256, 128))
    nt, nc = tp // tt, width // cw
    nb = cw // LRU_BLOCK_DIM

    def row_blk(b, j, t):
        return b * nt + t

    in_specs = [
        pl.BlockSpec((tt, cw), lambda b, j, t: (row_blk(b, j, t), j)),
        pl.BlockSpec((SUBLANES, cw),
                     lambda b, j, t: (jnp.maximum(row_blk(b, j, t) * (tt // SUBLANES) - 1, 0), j)),
        pl.BlockSpec((tt, cw), lambda b, j, t: (row_blk(b, j, t), nc + j)),
        pl.BlockSpec((CONV_WIDTH, cw), lambda b, j, t: (0, j)),
        pl.BlockSpec((1, cw), lambda b, j, t: (0, j)),
        pl.BlockSpec((nb, LRU_BLOCK_DIM, LRU_BLOCK_DIM), lambda b, j, t: (j, 0, 0)),
        pl.BlockSpec((1, cw), lambda b, j, t: (0, j)),
        pl.BlockSpec((nb, LRU_BLOCK_DIM, LRU_BLOCK_DIM), lambda b, j, t: (j, 0, 0)),
        pl.BlockSpec((1, cw), lambda b, j, t: (0, j)),
        pl.BlockSpec((1, cw), lambda b, j, t: (0, j)),
    ]
    return pl.pallas_call(
        functools.partial(_lru_kernel, tt=tt, cw=cw),
        grid=(batch, nc, nt),
        in_specs=in_specs,
        out_specs=pl.BlockSpec((tt, cw), lambda b, j, t: (row_blk(b, j, t), j)),
        out_shape=jax.ShapeDtypeStruct((m, width), BF16),
        scratch_shapes=[pltpu.VMEM((tt + SUBLANES, cw), F32), pltpu.VMEM((tt, cw), F32),
                        pltpu.VMEM((tt, cw), F32), pltpu.VMEM((tt, cw), F32),
                        pltpu.VMEM((SUBLANES, cw), F32)],
        compiler_params=pltpu.CompilerParams(
            dimension_semantics=("parallel", "parallel", "arbitrary"),
            vmem_limit_bytes=_vmem_limit(tt * cw * 4 * 12)),
        name="rglru",
    )(u, u, u, conv_w.astype(F32), conv_b.reshape(1, width).astype(F32),
      w_a.astype(BF16), b_a.reshape(1, width).astype(F32),
      w_x.astype(BF16), b_x.reshape(1, width).astype(F32), lam.reshape(1, width).astype(F32))


def _headnorm_kernel(*refs, rotary, q_scale, width):
    q_ref, k_ref, v_ref, qg_ref, kg_ref = refs[:5]
    pos = 5
    if rotary:
        c_ref, s1_ref, s2_ref = refs[pos:pos + 3]
        pos += 3
    qo_ref, ko_ref, vo_ref = refs[pos:pos + 3]

    def norm(x_ref, g_ref, o_ref, scale):
        x = x_ref[...]
        cols = []
        for g in range(width // LANES):
            xs = x[:, g * LANES:(g + 1) * LANES]
            y = xs * lax.rsqrt(jnp.mean(xs * xs, axis=-1, keepdims=True) + NORM_EPS) * g_ref[...]
            if rotary:
                y = (y * c_ref[...] + pltpu.roll(y, LANES - ROT_HALF, 1) * s1_ref[...]
                     + pltpu.roll(y, ROT_HALF, 1) * s2_ref[...])
            if scale != 1.0:
                y = y * scale
            cols.append(y.astype(o_ref.dtype))
        o_ref[...] = jnp.concatenate(cols, axis=1)

    norm(q_ref, qg_ref, qo_ref, q_scale)
    norm(k_ref, kg_ref, ko_ref, 1.0)
    vo_ref[...] = v_ref[...].astype(vo_ref.dtype)


def _headnorm(u, q_gain, k_gain, *, q_scale, tp, rot_tables=None):
    m = u.shape[0]
    width = 512
    ncb = GROUP_WIDTH // width
    tr = _tile(tp, (640, 384, 256, 128))
    nt = tp // tr
    rotary = rot_tables is not None
    in_specs = [
        pl.BlockSpec((tr, width), lambda i, j: (i, j)),
        pl.BlockSpec((tr, width), lambda i, j: (i, ncb + j)),
        pl.BlockSpec((tr, width), lambda i, j: (i, 2 * ncb + j)),
        pl.BlockSpec((1, LANES), lambda i, j: (0, 0)),
        pl.BlockSpec((1, LANES), lambda i, j: (0, 0)),
    ]
    args = [u, u, u, q_gain.reshape(1, LANES).astype(F32), k_gain.reshape(1, LANES).astype(F32)]
    if rotary:
        in_specs += [pl.BlockSpec((tr, LANES), lambda i, j: (i % nt, 0))] * 3
        args += list(rot_tables)
    out = jax.ShapeDtypeStruct((m, GROUP_WIDTH), BF16)
    return pl.pallas_call(
        functools.partial(_headnorm_kernel, rotary=rotary, q_scale=q_scale, width=width),
        grid=(m // tr, ncb),
        in_specs=in_specs,
        out_specs=[pl.BlockSpec((tr, width), lambda i, j: (i, j))] * 3,
        out_shape=[out, out, out],
        compiler_params=pltpu.CompilerParams(
            dimension_semantics=("parallel", "parallel"),
            vmem_limit_bytes=_vmem_limit(tr * width * 24)),
        name="headnorm_rot" if rotary else "headnorm",
    )(*args)


def _rotary_tables(tp):
    inv_freq = ROPE_THETA ** (-jnp.arange(ROT_HALF, dtype=F32) / ROT_HALF)
    ang = jnp.arange(tp, dtype=F32)[:, None] * inv_freq[None, :]
    cos, sin = jnp.cos(ang), jnp.sin(ang)
    ones = jnp.ones((tp, LANES - 2 * ROT_HALF), F32)
    zeros = jnp.zeros((tp, LANES - 2 * ROT_HALF), F32)
    zh = jnp.zeros((tp, ROT_HALF), F32)
    c = jnp.concatenate([cos, cos, ones], axis=1)
    s1 = jnp.concatenate([-sin, zh, zeros], axis=1)
    s2 = jnp.concatenate([zh, sin, zeros], axis=1)
    return c, s1, s2


def _softmax_step(s, v, m_ref, l_ref, acc_ref):
    m_prev = m_ref[...]
    m_new = jnp.maximum(m_prev, jnp.max(s, axis=1, keepdims=True))
    alpha = jnp.exp2(m_prev - m_new)
    p = jnp.exp2(s - m_new)
    if l_ref is not None:
        l_ref[...] = alpha * l_ref[...] + jnp.sum(p, axis=1, keepdims=True)
    acc_ref[...] = alpha * acc_ref[...] + _dot(p.astype(BF16), v)
    m_ref[...] = m_new


def _causal_mask(s):
    row = lax.broadcasted_iota(jnp.int32, s.shape, 0)
    col = lax.broadcasted_iota(jnp.int32, s.shape, 1)
    return jnp.where(col <= row, s, MASK_VALUE)


def _kv_sweep(step, qi):
    def body(j, carry):
        for g in range(KV_GROUP):
            step(KV_GROUP * j + g, False)
        return carry

    lax.fori_loop(0, qi // KV_GROUP, body, 0)
    rem = qi % KV_GROUP
    for g in range(1, KV_GROUP):
        @pl.when(rem >= g)
        def _(g=g):
            step(qi - rem + g - 1, False)

    step(qi, True)


def _diff_flash_kernel(q1_ref, q2_ref, k1_ref, k2_ref, v_ref, lq1_ref, lk1_ref, lq2_ref, lk2_ref, sg_ref,
                       o_ref, m1, l1, acc1, m2, l2, acc2, *, lambda_init, blk):
    qi = pl.program_id(2)
    sets = ((q1_ref, k1_ref, m1, l1, acc1), (q2_ref, k2_ref, m2, l2, acc2))
    for _, _, m, l, acc in sets:
        m[...] = jnp.full_like(m, MASK_VALUE)
        l[...] = jnp.zeros_like(l)
        acc[...] = jnp.zeros_like(acc)

    def step(j, masked):
        r0 = pl.multiple_of(j * blk, blk)
        v = v_ref[pl.ds(r0, blk), :]
        for q_ref, k_ref, m, l, acc in sets:
            s = _nt(q_ref[...], k_ref[pl.ds(r0, blk), :])
            _softmax_step(_causal_mask(s) if masked else s, v, m, l, acc)

    _kv_sweep(step, qi)
    lam = (jnp.exp(jnp.sum(lq1_ref[...] * lk1_ref[...], axis=1, keepdims=True))
           - jnp.exp(jnp.sum(lq2_ref[...] * lk2_ref[...], axis=1, keepdims=True)) + lambda_init)
    o = acc1[...] / l1[...] - lam * (acc2[...] / l2[...])
    o = o * lax.rsqrt(jnp.mean(o * o, axis=-1, keepdims=True) + NORM_EPS)
    o_ref[...] = (o * sg_ref[...] * (1.0 - lambda_init)).astype(o_ref.dtype)


def _diff_attention(qn, kn, vb, lq1, lk1, lq2, lk2, sub_gain, *, lambda_init, batch, tp):
    m = qn.shape[0]
    blk = _tile(tp, (640, 384, 256, 128))
    nq = tp // blk
    hd, vd = DIFF_HEAD_DIM, 2 * DIFF_HEAD_DIM
    vec = lambda p: p.reshape(1, -1).astype(F32)
    pspec = lambda w: pl.BlockSpec((1, w), lambda b, h, i: (0, 0))
    return pl.pallas_call(
        functools.partial(_diff_flash_kernel, lambda_init=lambda_init, blk=blk),
        grid=(batch, DIFF_HEADS, nq),
        in_specs=[
            pl.BlockSpec((blk, hd), lambda b, h, i: (b * nq + i, 2 * h)),
            pl.BlockSpec((blk, hd), lambda b, h, i: (b * nq + i, 2 * h + 1)),
            pl.BlockSpec((tp, hd), lambda b, h, i: (b, 2 * h)),
            pl.BlockSpec((tp, hd), lambda b, h, i: (b, 2 * h + 1)),
            pl.BlockSpec((tp, vd), lambda b, h, i: (b, h)),
            pspec(hd), pspec(hd), pspec(hd), pspec(hd), pspec(vd),
        ],
        out_specs=pl.BlockSpec((blk, vd), lambda b, h, i: (b * nq + i, h)),
        out_shape=jax.ShapeDtypeStruct((m, GROUP_WIDTH), BF16),
        scratch_shapes=[pltpu.VMEM((blk, 1), F32), pltpu.VMEM((blk, 1), F32), pltpu.VMEM((blk, vd), F32),
                        pltpu.VMEM((blk, 1), F32), pltpu.VMEM((blk, 1), F32), pltpu.VMEM((blk, vd), F32)],
        compiler_params=pltpu.CompilerParams(
            dimension_semantics=("parallel", "parallel", "arbitrary"),
            vmem_limit_bytes=_vmem_limit(tp * (2 * hd + vd) * 2 + blk * blk * 4 * 6)),
        name="diff_attention",
    )(qn, qn, kn, kn, vb, vec(lq1), vec(lk1), vec(lq2), vec(lk2), vec(sub_gain))


def _fox_flash_kernel(q_ref, k_ref, v_ref, c_ref, o_ref, vaug_ref, m_ref, acc_ref, *, blk):
    qi = pl.program_id(2)
    hd = FOX_HEAD_DIM

    @pl.when(qi == 0)
    def _():
        vaug_ref[:, 0:hd] = v_ref[...]
        vaug_ref[:, hd:2 * hd] = jnp.ones((vaug_ref.shape[0], hd), vaug_ref.dtype)

    m_ref[...] = jnp.full_like(m_ref, MASK_VALUE)
    acc_ref[...] = jnp.zeros_like(acc_ref)
    cq = c_ref[0, qi][:, 0:1]

    def step(j, masked):
        r0 = pl.multiple_of(j * blk, blk)
        s = _nt(q_ref[...], k_ref[pl.ds(r0, blk), :]) + (cq - c_ref[0, j]) * LOG2E
        _softmax_step(_causal_mask(s) if masked else s, vaug_ref[pl.ds(r0, blk), :], m_ref, None, acc_ref)

    _kv_sweep(step, qi)
    acc = acc_ref[...]
    o_ref[...] = (acc[:, 0:hd] / acc[:, hd:2 * hd]).astype(o_ref.dtype)


def _fox_attention(qn, kn, vb, c_rows, *, batch, tp):
    m = qn.shape[0]
    blk = _tile(tp, (640, 384, 256, 128))
    nq = tp // blk
    hd = FOX_HEAD_DIM
    c_blocks = c_rows.reshape(batch * FOX_HEADS, nq, 1, blk)
    return pl.pallas_call(
        functools.partial(_fox_flash_kernel, blk=blk),
        grid=(batch, FOX_HEADS, nq),
        in_specs=[
            pl.BlockSpec((blk, hd), lambda b, h, i: (b * nq + i, h)),
            pl.BlockSpec((tp, hd), lambda b, h, i: (b, h)),
            pl.BlockSpec((tp, hd), lambda b, h, i: (b, h)),
            pl.BlockSpec((1, nq, 1, blk), lambda b, h, i: (b * FOX_HEADS + h, 0, 0, 0)),
        ],
        out_specs=pl.BlockSpec((blk, hd), lambda b, h, i: (b * nq + i, h)),
        out_shape=jax.ShapeDtypeStruct((m, GROUP_WIDTH), BF16),
        scratch_shapes=[pltpu.VMEM((tp, 2 * hd), BF16), pltpu.VMEM((blk, 1), F32),
                        pltpu.VMEM((blk, 2 * hd), F32)],
        compiler_params=pltpu.CompilerParams(
            dimension_semantics=("parallel", "parallel", "arbitrary"),
            vmem_limit_bytes=_vmem_limit(tp * hd * 2 * 4 + blk * blk * 4 * 6)),
        name="fox_attention",
    )(qn, kn, vb, c_blocks)


def _fox_gate_kernel(hn_ref, wf_ref, bias_ref, c_ref, carry_ref, *, tc):
    t = pl.program_id(1)

    @pl.when(t == 0)
    def _():
        carry_ref[...] = jnp.zeros_like(carry_ref)

    f = _nt(wf_ref[...], hn_ref[...]) + bias_ref[...]
    log_f = jnp.minimum(f, 0.0) - jnp.log1p(jnp.exp(-jnp.abs(f)))
    upper = (lax.broadcasted_iota(jnp.int32, (tc, tc), 0)
             <= lax.broadcasted_iota(jnp.int32, (tc, tc), 1)).astype(BF16)
    hi, mid, lo = _split3(log_f)
    c = _dot(hi, upper) + _dot(mid, upper) + _dot(lo, upper) + carry_ref[:, 0:1]
    c_ref[0] = c
    carry_ref[...] = jnp.broadcast_to(c[:, tc - 1:tc], carry_ref.shape)


def _fox_gates(hn, wf_t, f_bias, *, batch, tp):
    d = hn.shape[1]
    tc = _tile(tp, (640, 384, 256, 128))
    nt = tp // tc
    return pl.pallas_call(
        functools.partial(_fox_gate_kernel, tc=tc),
        grid=(batch, nt),
        in_specs=[pl.BlockSpec((tc, d), lambda b, t: (b * nt + t, 0)),
                  pl.BlockSpec((FOX_HEADS, d), lambda b, t: (0, 0)),
                  pl.BlockSpec((FOX_HEADS, 1), lambda b, t: (0, 0))],
        out_specs=pl.BlockSpec((1, FOX_HEADS, tc), lambda b, t: (b, 0, t)),
        out_shape=jax.ShapeDtypeStruct((batch, FOX_HEADS, tp), F32),
        scratch_shapes=[pltpu.VMEM((FOX_HEADS, LANES), F32)],
        compiler_params=pltpu.CompilerParams(
            dimension_semantics=("parallel", "arbitrary"),
            vmem_limit_bytes=_vmem_limit(tc * d * 2 + tc * tc * 8)),
        name="fox_gates",
    )(hn, wf_t, f_bias.reshape(FOX_HEADS, 1).astype(F32))


def _head_sum(x, ones_bd):
    cols = []
    for g in range(x.shape[1] // LANES):
        hi, lo = _split2(x[:, g * LANES:(g + 1) * LANES])
        cols.append(_dot(hi, ones_bd) + _dot(lo, ones_bd))
    return jnp.concatenate(cols, axis=1)


def _head_ones():
    r = lax.broadcasted_iota(jnp.int32, (LANES, LANES), 0) < RWKV_HEAD_DIM
    c = lax.broadcasted_iota(jnp.int32, (LANES, LANES), 1) < RWKV_HEAD_DIM
    return jnp.where(r == c, 1.0, 0.0).astype(BF16)


def _rwkv_pre_kernel(zr_ref, zrh_ref, zk_ref, zkh_ref, zv_ref, zvh_ref, zl_ref, zlh_ref,
                     mur_ref, muk_ref, muv_ref, mul_ref, w0_ref, wup_ref, a0_ref, aup_ref, gup_ref,
                     kk_ref, ka_ref,
                     r_o, k_o, v_o, a_o, b_o, lw_o, g_o, xs_ref, ls_ref, *, tr, nt):
    first = (pl.program_id(0) % nt) == 0

    def shift(x_ref, h_ref, mu_ref, s_ref):
        s_ref[0:SUBLANES, :] = jnp.where(first, 0.0, h_ref[...])
        s_ref[SUBLANES:SUBLANES + tr, :] = x_ref[...]
        x = x_ref[...]
        return x + (s_ref[SUBLANES - 1:SUBLANES - 1 + tr, :] - x) * mu_ref[...]

    zl = shift(zl_ref, zlh_ref, mul_ref, ls_ref)
    wd = zl[:, 0:DECAY_LORA]
    ad = zl[:, DECAY_LORA:DECAY_LORA + AAA_LORA]
    gd = zl[:, DECAY_LORA + AAA_LORA:]
    r = shift(zr_ref, zrh_ref, mur_ref, xs_ref)
    k = shift(zk_ref, zkh_ref, muk_ref, xs_ref)
    v = shift(zv_ref, zvh_ref, muv_ref, xs_ref)

    w_log = -_softplus(-(w0_ref[...] + _dot(jnp.tanh(wd).astype(BF16), wup_ref[...]))) - 0.5
    a = jax.nn.sigmoid(a0_ref[...] + _dot(ad.astype(BF16), aup_ref[...]))
    g = _dot(jax.nn.sigmoid(gd).astype(BF16), gup_ref[...])
    kx = k * kk_ref[...]
    kk = kx / jnp.maximum(jnp.sqrt(_head_sum(kx * kx, _head_ones())), 1e-12)
    r_o[...] = r
    k_o[...] = k * (1.0 + (a - 1.0) * ka_ref[...])
    v_o[...] = v
    a_o[...] = -kk
    b_o[...] = kk * a
    lw_o[...] = -jnp.exp(w_log)
    g_o[...] = g


def _rwkv_pre(z, mu, w0, w_up, a0, a_up, g_up, k_k, k_a, *, tp):
    m = z.shape[0]
    width = GROUP_WIDTH
    cw = 512
    ncb = width // cw
    lw = RWKV_LORA_PAD
    tr = _tile(tp, (320, 384, 256, 128))
    nt = tp // tr
    hb = tr // SUBLANES

    def halo(i):
        return jnp.maximum(i * hb - 1, 0)

    mu2 = mu.reshape(1, -1).astype(F32)
    row = lambda p: p.reshape(1, width).astype(F32)
    g_up_p = jnp.zeros((lw - DECAY_LORA - AAA_LORA, width), BF16).at[:GATE_LORA].set(g_up.astype(BF16))
    in_specs = []
    for c in range(3):
        in_specs.append(pl.BlockSpec((tr, cw), lambda i, j, c=c: (i, c * ncb + j)))
        in_specs.append(pl.BlockSpec((SUBLANES, cw), lambda i, j, c=c: (halo(i), c * ncb + j)))
    in_specs.append(pl.BlockSpec((tr, lw), lambda i, j: (i, 3 * width // lw)))
    in_specs.append(pl.BlockSpec((SUBLANES, lw), lambda i, j: (halo(i), 3 * width // lw)))
    for c in range(3):
        in_specs.append(pl.BlockSpec((1, cw), lambda i, j, c=c: (0, c * ncb + j)))
    in_specs.append(pl.BlockSpec((1, lw), lambda i, j: (0, 3 * width // lw)))
    in_specs += [
        pl.BlockSpec((1, cw), lambda i, j: (0, j)),
        pl.BlockSpec((DECAY_LORA, cw), lambda i, j: (0, j)),
        pl.BlockSpec((1, cw), lambda i, j: (0, j)),
        pl.BlockSpec((AAA_LORA, cw), lambda i, j: (0, j)),
        pl.BlockSpec((lw - DECAY_LORA - AAA_LORA, cw), lambda i, j: (0, j)),
        pl.BlockSpec((1, cw), lambda i, j: (0, j)),
        pl.BlockSpec((1, cw), lambda i, j: (0, j)),
    ]
    out = jax.ShapeDtypeStruct((m, width), F32)
    return pl.pallas_call(
        functools.partial(_rwkv_pre_kernel, tr=tr, nt=nt),
        grid=(m // tr, ncb),
        in_specs=in_specs,
        out_specs=[pl.BlockSpec((tr, cw), lambda i, j: (i, j))] * 7,
        out_shape=[out] * 7,
        scratch_shapes=[pltpu.VMEM((tr + SUBLANES, cw), F32), pltpu.VMEM((tr + SUBLANES, lw), F32)],
        compiler_params=pltpu.CompilerParams(
            dimension_semantics=("parallel", "parallel"),
            vmem_limit_bytes=_vmem_limit(tr * cw * 4 * 16 + tr * lw * 4 * 3)),
        name="rwkv_pre",
    )(z, z, z, z, z, z, z, z, mu2, mu2, mu2, mu2, row(w0), w_up.astype(BF16), row(a0),
      a_up.astype(BF16), g_up_p, row(k_k), row(k_a))


def _apply_and_square(x, r, square):
    n = x.shape[1]
    xh, xl = _split2(x)
    rh, rl = _split2(r)
    if square:
        rh = jnp.concatenate([xh, rh], axis=1)
        rl = jnp.concatenate([xl, rl], axis=1)
    p = _dot(xh, rh) + _dot(xh, rl) + _dot(xl, rh)
    if square:
        return p[:, 0:n], r + p[:, n:]
    return x, r + p


def _rwkv_scan_kernel(r_ref, k_ref, v_ref, a_ref, b_ref, lw_ref, y_ref, s_ref, *, tt, pairs):
    C = RWKV_CHUNK
    t = pl.program_id(2)

    @pl.when(t == 0)
    def _():
        s_ref[...] = jnp.zeros_like(s_ref)

    lane_top = lax.broadcasted_iota(jnp.int32, (C, LANES), 1) < RWKV_HEAD_DIM
    ri = lax.broadcasted_iota(jnp.int32, (2 * C, 2 * C), 0)
    ci = lax.broadcasted_iota(jnp.int32, (2 * C, 2 * C), 1)
    strict = (ci & (C - 1)) < (ri & (C - 1))
    incl = (ci & (C - 1)) <= (ri & (C - 1))
    strict2 = jnp.concatenate([strict, strict], axis=1)
    incl2 = jnp.concatenate([incl, incl], axis=1)
    lower = (lax.broadcasted_iota(jnp.int32, (C, C), 1)
             <= lax.broadcasted_iota(jnp.int32, (C, C), 0)).astype(BF16)

    def stack(x):
        return jnp.concatenate([jnp.where(lane_top, x, 0.0), jnp.where(lane_top, 0.0, x)], axis=0)

    def chunk(c, carry):
        r0 = pl.multiple_of(c * C, C)
        P = range(pairs)
        sls = [slice(p * LANES, (p + 1) * LANES) for p in P]
        lw = [lw_ref[pl.ds(r0, C), sl] for sl in sls]
        sp = [_split3(x) for x in lw]
        g_in = [_dot(lower, hi) + _dot(lower, mid) + _dot(lower, lo) for hi, mid, lo in sp]
        g_last = [g[C - 1:C, :] for g in g_in]
        e_in = [jnp.exp(g) for g in g_in]
        e_ex = [jnp.exp(g - l) for g, l in zip(g_in, lw)]
        e_neg = [jnp.exp(-g) for g in g_in]
        e_rem = [jnp.exp(gl - g) for g, gl in zip(g_in, g_last)]
        r = [r_ref[pl.ds(r0, C), sl] for sl in sls]
        k = [k_ref[pl.ds(r0, C), sl] for sl in sls]
        a = [a_ref[pl.ds(r0, C), sl] for sl in sls]
        b = [b_ref[pl.ds(r0, C), sl] for sl in sls]
        xa = [stack(a[p] * e_ex[p]).astype(BF16) for p in P]
        xr = [stack(r[p] * e_in[p]).astype(BF16) for p in P]
        ybk = [jnp.concatenate([stack(b[p] * e_neg[p]), stack(k[p] * e_neg[p])], axis=0).astype(BF16) for p in P]
        vs = [stack(v_ref[pl.ds(r0, C), sl]).astype(BF16) for sl in sls]
        bk = [jnp.concatenate([stack(b[p] * e_rem[p]), stack(k[p] * e_rem[p])], axis=0).astype(BF16) for p in P]
        lp = [jnp.where(strict2, _nt(xa[p], ybk[p]), 0.0) for p in P]
        mp = [jnp.where(incl2, _nt(xr[p], ybk[p]), 0.0).astype(BF16) for p in P]
        s0 = [s_ref[p] for p in P]
        s0b = [s.astype(BF16) for s in s0]
        x = [l[:, 0:2 * C] for l in lp]
        u = [_nt(xa[p], s0b[p]) + _dot(lp[p][:, 2 * C:].astype(BF16), vs[p]) for p in P]
        levels = int(math.log2(C))
        for lvl in range(levels):
            x, u = zip(*[_apply_and_square(xx, uu, lvl < levels - 1) for xx, uu in zip(x, u)])
        uv = [jnp.concatenate([u[p].astype(BF16), vs[p]], axis=0) for p in P]
        ys = [_nt(xr[p], s0b[p]) + _dot(mp[p], uv[p]) for p in P]
        for p in P:
            y_ref[pl.ds(r0, C), sls[p]] = ys[p][0:C, :] + ys[p][C:2 * C, :]
        for p in P:
            s_ref[p] = s0[p] * jnp.exp(g_last[p]) + _tn(uv[p], bk[p])
        return carry

    lax.fori_loop(0, tt // C, chunk, 0)


def _rwkv_scan(r, k, v, a, b, lw, *, batch, tp):
    m = r.shape[0]
    pairs = 8
    cw = pairs * LANES
    tt = _tile(tp, (640, 384, 256, 128))
    nt, nc = tp // tt, GROUP_WIDTH // cw
    spec = pl.BlockSpec((tt, cw), lambda bi, j, t: (bi * nt + t, j))
    return pl.pallas_call(
        functools.partial(_rwkv_scan_kernel, tt=tt, pairs=pairs),
        grid=(batch, nc, nt),
        in_specs=[spec] * 6,
        out_specs=spec,
        out_shape=jax.ShapeDtypeStruct((m, GROUP_WIDTH), F32),
        scratch_shapes=[pltpu.VMEM((pairs, LANES, LANES), F32)],
        compiler_params=pltpu.CompilerParams(
            dimension_semantics=("parallel", "parallel", "arbitrary"),
            vmem_limit_bytes=_vmem_limit(tt * cw * 4 * 8)),
        name="rwkv_scan",
    )(r, k, v, a, b, lw)


def _rwkv_post_kernel(y_ref, r_ref, k_ref, v_ref, g_ref, rk_ref, gw_ref, gb_ref, o_ref):
    ones_bd = _head_ones()
    inv_n = 1.0 / RWKV_HEAD_DIM
    y = y_ref[...]
    mean = _head_sum(y, ones_bd) * inv_n
    d = y - mean
    var = _head_sum(d * d, ones_bd) * inv_n
    yn = d * lax.rsqrt(var + RWKV_GN_EPS) * gw_ref[...] + gb_ref[...]
    bonus = _head_sum(r_ref[...] * k_ref[...] * rk_ref[...], ones_bd) * v_ref[...]
    o_ref[...] = ((yn + bonus) * g_ref[...]).astype(o_ref.dtype)


def _rwkv_post(y, r, k, v, g, r_k, gn_w, gn_b, *, tp):
    m = y.shape[0]
    cw = 512
    tr = _tile(tp, (640, 384, 256, 128))
    spec = pl.BlockSpec((tr, cw), lambda i, j: (i, j))
    pspec = pl.BlockSpec((1, cw), lambda i, j: (0, j))
    row = lambda p: p.reshape(1, GROUP_WIDTH).astype(F32)
    return pl.pallas_call(
        _rwkv_post_kernel,
        grid=(m // tr, GROUP_WIDTH // cw),
        in_specs=[spec] * 5 + [pspec] * 3,
        out_specs=spec,
        out_shape=jax.ShapeDtypeStruct((m, GROUP_WIDTH), BF16),
        compiler_params=pltpu.CompilerParams(
            dimension_semantics=("parallel", "parallel"),
            vmem_limit_bytes=_vmem_limit(tr * cw * 4 * 10)),
        name="rwkv_post",
    )(y, r, k, v, g, row(r_k), row(gn_w), row(gn_b))


def _mlp(h, f_gain, f_up, f_down, tm):
    hn = _rmsnorm_rows(h, f_gain)
    hid = _matmul([hn], f_up.astype(BF16), n_off=0, n=D_FF, tm=tm, tn=512, out_dtype=BF16,
                  relu2=True, name="ffn_up")
    return _matmul([hid], f_down.astype(BF16), n_off=0, n=D_MODEL, tm=tm, tn=1024, tk=2048,
                   residual=h, name="ffn_down")


def kernel(x, meta_tokens, mix_norm_0, w_in_0, conv_w_0, conv_b_0, lru_wa_0, lru_ba_0, lru_wx_0, lru_bx_0, lru_lam_0, diff_q_gain_0, diff_k_gain_0, diff_lq1_0, diff_lk1_0, diff_lq2_0, diff_lk2_0, diff_sub_gain_0, w_out_0, ffn_norm_0, ffn_up_0, ffn_down_0, mix_norm_1, w_in_1, rwkv_mu_1, rwkv_w0_1, rwkv_w_up_1, rwkv_a0_1, rwkv_a_up_1, rwkv_g_up_1, rwkv_k_k_1, rwkv_k_a_1, rwkv_r_k_1, rwkv_gn_w_1, rwkv_gn_b_1, fox_q_gain_1, fox_k_gain_1, fox_f_bias_1, w_out_1, ffn_norm_1, ffn_up_1, ffn_down_1):
    batch, seq, d = x.shape
    assert d == D_MODEL
    t_real = seq + N_META
    tp = -(-t_real // LANES) * LANES
    m = batch * tp
    tm = _tile(m, (1280, 768, 640, 512, 384, 256, 128))
    gw = GROUP_WIDTH

    meta = jnp.broadcast_to(meta_tokens[None].astype(x.dtype), (batch, N_META, d))
    pad = jnp.zeros((batch, tp - t_real, d), x.dtype)
    h = jnp.concatenate([meta, x, pad], axis=1).reshape(m, d)

    hn = _rmsnorm_rows(h, mix_norm_0)
    w_in = w_in_0.astype(BF16)
    u_lru = _matmul([hn], w_in, n_off=0, n=2 * gw, tm=tm, tn=512, name="in_proj0_lru")
    u_att = _matmul([hn], w_in, n_off=2 * gw, n=3 * gw, tm=tm, tn=512, name="in_proj0_att")
    y_a = _lru_branch(u_lru, conv_w_0, conv_b_0, lru_wa_0, lru_ba_0, lru_wx_0, lru_bx_0, lru_lam_0,
                      batch=batch, tp=tp)
    qn, kn, vb = _headnorm(u_att, diff_q_gain_0, diff_k_gain_0, q_scale=DIFF_HEAD_DIM ** -0.5 * LOG2E, tp=tp,
                           rot_tables=_rotary_tables(tp))
    lambda_init = 0.8 - 0.6 * math.exp(-0.3 * 0)
    y_b = _diff_attention(qn, kn, vb, diff_lq1_0, diff_lk1_0, diff_lq2_0, diff_lk2_0, diff_sub_gain_0,
                          lambda_init=lambda_init, batch=batch, tp=tp)
    h = _matmul([y_a, y_b], w_out_0.astype(BF16), n_off=0, n=d, tm=tm, tn=512, residual=h,
                name="out_proj0")
    h = _mlp(h, ffn_norm_0, ffn_up_0, ffn_down_0, tm)

    hn = _rmsnorm_rows(h, mix_norm_1)
    slab = 3 * gw + DECAY_LORA + AAA_LORA + GATE_LORA
    zw = 3 * gw + RWKV_LORA_PAD
    w_z = jnp.zeros((d, zw), BF16).at[:, :slab].set(w_in_1[:, :slab].astype(BF16))
    w_f = w_in_1[:, slab:slab + 3 * gw].astype(BF16)
    wf_t = w_in_1[:, slab + 3 * gw:].T.astype(BF16)
    mu = jnp.zeros((zw,), F32).at[:slab].set(rwkv_mu_1.astype(F32))
    z = _matmul([hn], w_z, n_off=0, n=zw, tm=tm, tn=512, name="in_proj1_rwkv")
    u_fox = _matmul([hn], w_f, n_off=0, n=3 * gw, tm=tm, tn=512, name="in_proj1_fox")
    c_rows = _fox_gates(hn, wf_t, fox_f_bias_1, batch=batch, tp=tp)

    r, k32, v, a_neg, b_vec, log_w, g = _rwkv_pre(z, mu, rwkv_w0_1, rwkv_w_up_1, rwkv_a0_1, rwkv_a_up_1,
                                                  rwkv_g_up_1, rwkv_k_k_1, rwkv_k_a_1, tp=tp)
    y = _rwkv_scan(r, k32, v, a_neg, b_vec, log_w, batch=batch, tp=tp)
    y_c = _rwkv_post(y, r, k32, v, g, rwkv_r_k_1, rwkv_gn_w_1, rwkv_gn_b_1, tp=tp)

    fq, fk, fv = _headnorm(u_fox, fox_q_gain_1, fox_k_gain_1, q_scale=FOX_HEAD_DIM ** -0.5 * LOG2E, tp=tp)
    y_d = _fox_attention(fq, fk, fv, c_rows, batch=batch, tp=tp)
    h = _matmul([y_c, y_d], w_out_1.astype(BF16), n_off=0, n=d, tm=tm, tn=512, residual=h,
                name="out_proj1")
    h = _mlp(h, ffn_norm_1, ffn_up_1, ffn_down_1, tm)

    return h.reshape(batch, tp, d)[:, N_META:N_META + seq]
```

```python
import functools
import math

import jax
import jax.numpy as jnp
from jax import lax
from jax.experimental import pallas as pl
from jax.experimental.pallas import tpu as pltpu

F32 = jnp.float32
BF16 = jnp.bfloat16

D_MODEL = 4096
N_META = 16
GROUP_WIDTH = D_MODEL // 2
D_FF = 4 * D_MODEL
NORM_EPS = 1e-6
ROPE_THETA = 500000.0

LRU_BLOCK_DIM = 128
CONV_WIDTH = 4
LRU_C = 8.0

DIFF_HEAD_DIM = 128
DIFF_HEADS = GROUP_WIDTH // (2 * DIFF_HEAD_DIM)
ROT_HALF = DIFF_HEAD_DIM // 8

RWKV_HEAD_DIM = 64
RWKV_CHUNK = 64
DECAY_LORA = 128
AAA_LORA = 128
GATE_LORA = 480
RWKV_LORA_PAD = 1024
RWKV_GN_EPS = 64e-5
RWKV_PRECISE_LEVELS = 2

FOX_HEAD_DIM = 128
FOX_HEADS = GROUP_WIDTH // FOX_HEAD_DIM

LANES = 128
SUBLANES = 8
VMEM_CAP_BYTES = 56 * 1024 * 1024
MASK_VALUE = -1e30
LOG2E = math.log2(math.e)
KV_GROUP = 2


def _tile(n, prefs):
    for p in prefs:
        if n % p == 0:
            return p
    raise ValueError(f"no tile of {prefs} divides {n}")


def _vmem_limit(block_bytes):
    return int(min(VMEM_CAP_BYTES, 2 * block_bytes + (8 << 20)))


def _nt(a, b):
    return lax.dot_general(a, b, (((1,), (1,)), ((), ())), preferred_element_type=F32)


def _tn(a, b):
    return lax.dot_general(a, b, (((0,), (0,)), ((), ())), preferred_element_type=F32)


def _dot(a, b):
    return jnp.dot(a, b, preferred_element_type=F32)


def _split2(x):
    hi = x.astype(BF16)
    lo = (x - hi.astype(F32)).astype(BF16)
    return hi, lo


def _split3(x):
    hi = x.astype(BF16)
    r1 = x - hi.astype(F32)
    mid = r1.astype(BF16)
    lo = (r1 - mid.astype(F32)).astype(BF16)
    return hi, mid, lo


def _sigmoid(x):
    return 0.5 * jnp.tanh(0.5 * x) + 0.5


def _softplus(x):
    return jnp.maximum(x, 0.0) + jnp.log1p(jnp.exp(-jnp.abs(x)))


def _rmsnorm_kernel(x_ref, g_ref, o_ref):
    x = x_ref[...]
    y = x * lax.rsqrt(jnp.mean(x * x, axis=-1, keepdims=True) + NORM_EPS)
    o_ref[...] = (y * g_ref[...]).astype(o_ref.dtype)


def _rmsnorm_rows(x, gain):
    m, d = x.shape
    tr = _tile(m, (320, 256, 128))
    return pl.pallas_call(
        _rmsnorm_kernel,
        grid=(m // tr,),
        in_specs=[pl.BlockSpec((tr, d), lambda i: (i, 0)),
                  pl.BlockSpec((1, d), lambda i: (0, 0))],
        out_specs=pl.BlockSpec((tr, d), lambda i: (i, 0)),
        out_shape=jax.ShapeDtypeStruct((m, d), BF16),
        compiler_params=pltpu.CompilerParams(
            dimension_semantics=("parallel",),
            vmem_limit_bytes=_vmem_limit(tr * d * 6)),
        name="rmsnorm_rows",
    )(x, gain.reshape(1, d).astype(F32))


def _head_norm(x, gain, scale, rot, out_dtype):
    cols = []
    for g in range(x.shape[1] // LANES):
        xs = x[:, g * LANES:(g + 1) * LANES]
        y = xs * lax.rsqrt(jnp.mean(xs * xs, axis=-1, keepdims=True) + NORM_EPS) * gain
        if rot is not None:
            c, s1, s2 = rot
            y = y * c + pltpu.roll(y, LANES - ROT_HALF, 1) * s1 + pltpu.roll(y, ROT_HALF, 1) * s2
        if scale != 1.0:
            y = y * scale
        cols.append(y.astype(out_dtype))
    return jnp.concatenate(cols, axis=1)


def _mm_kernel(*refs, n_a, k_sizes, nk, has_res, relu2, norm_scale):
    a_refs = refs[:n_a]
    w_ref = refs[n_a]
    pos = n_a + 1
    res_ref = gain_ref = None
    if has_res:
        res_ref = refs[pos]
        pos += 1
    if norm_scale is not None:
        gain_ref = refs[pos]
        pos += 1
    o_ref = refs[pos]
    acc_ref = refs[pos + 1] if nk > 1 else None

    def partial_product():
        acc = None
        off = 0
        for a_ref, ks in zip(a_refs, k_sizes):
            p = _dot(a_ref[...], w_ref[off:off + ks, :].astype(BF16))
            acc = p if acc is None else acc + p
            off += ks
        return acc

    def finish(acc):
        if relu2:
            acc = jnp.square(jnp.maximum(acc, 0.0))
        if has_res:
            acc = acc + res_ref[...]
        if norm_scale is not None:
            o_ref[...] = _head_norm(acc, gain_ref[...], norm_scale, None, o_ref.dtype)
        else:
            o_ref[...] = acc.astype(o_ref.dtype)

    if nk == 1:
        finish(partial_product())
        return

    k = pl.program_id(2)

    @pl.when(k == 0)
    def _():
        acc_ref[...] = jnp.zeros_like(acc_ref)

    acc_ref[...] += partial_product()

    @pl.when(k == nk - 1)
    def _():
        finish(acc_ref[...])


def _matmul(a_list, w, *, n_off, n, tm, tn, tk=None, out_dtype=F32, residual=None, relu2=False,
            head_norm=None, name):
    m = a_list[0].shape[0]
    k_sizes = tuple(a.shape[1] for a in a_list)
    k_total = sum(k_sizes)
    if tk is None:
        tk = k_total
    assert len(a_list) == 1 or tk == k_total
    assert m % tm == 0 and n % tn == 0 and n_off % tn == 0 and k_total % tk == 0
    nk = k_total // tk
    j_off = n_off // tn
    blk_k = (tk,) if len(a_list) == 1 else k_sizes
    in_specs = [pl.BlockSpec((tm, bk), lambda i, j, k: (i, k)) for bk in blk_k]
    in_specs.append(pl.BlockSpec((tk, tn), lambda i, j, k: (k, j + j_off)))
    args = list(a_list) + [w]
    if residual is not None:
        in_specs.append(pl.BlockSpec((tm, tn), lambda i, j, k: (i, j)))
        args.append(residual)
    norm_scale = None
    if head_norm is not None:
        gain, norm_scale = head_norm
        in_specs.append(pl.BlockSpec((1, LANES), lambda i, j, k: (0, 0)))
        args.append(gain.reshape(1, LANES).astype(F32))
    out_bytes = jnp.dtype(out_dtype).itemsize
    block_bytes = (tm * tk * 2 + tk * tn * w.dtype.itemsize + tm * tn * out_bytes
                   + (tm * tn * 4 if residual is not None else 0))
    scratch = [pltpu.VMEM((tm, tn), F32)] if nk > 1 else []
    return pl.pallas_call(
        functools.partial(_mm_kernel, n_a=len(a_list), k_sizes=blk_k, nk=nk,
                          has_res=residual is not None, relu2=relu2, norm_scale=norm_scale),
        grid=(m // tm, n // tn, nk),
        in_specs=in_specs,
        out_specs=pl.BlockSpec((tm, tn), lambda i, j, k: (i, j)),
        out_shape=jax.ShapeDtypeStruct((m, n), out_dtype),
        scratch_shapes=scratch,
        compiler_params=pltpu.CompilerParams(
            dimension_semantics=("parallel", "parallel", "arbitrary"),
            vmem_limit_bytes=_vmem_limit(block_bytes + tm * tn * 2)),
        name=name,
    )(*args)


def _lru_kernel(xb_ref, xh_ref, gb_ref, cw_ref, cb_ref, wa_ref, ba_ref, wx_ref, bx_ref, lam_ref,
                o_ref, a_s, u_s, h_s, hprev_ref, *, tt, cw):
    t = pl.program_id(2)

    @pl.when(t == 0)
    def _():
        hprev_ref[...] = jnp.zeros_like(hprev_ref)

    x = xb_ref[...]
    halo = jnp.where(t == 0, 0.0, xh_ref[...])
    w = cw_ref[...]
    row = lax.broadcasted_iota(jnp.int32, (SUBLANES, cw), 0)
    xc = cb_ref[...] + x * w[CONV_WIDTH - 1:CONV_WIDTH, :]
    for s in range(1, CONV_WIDTH):
        rolled = pltpu.roll(x, s, 0)
        top = jnp.where(row < s, pltpu.roll(halo, s, 0), rolled[0:SUBLANES, :])
        shifted = jnp.concatenate([top, rolled[SUBLANES:, :]], axis=0)
        xc = xc + shifted * w[CONV_WIDTH - 1 - s:CONV_WIDTH - s, :]

    gate_a, gate_x = [], []
    for n in range(cw // LRU_BLOCK_DIM):
        xh = xc[:, n * LRU_BLOCK_DIM:(n + 1) * LRU_BLOCK_DIM].astype(BF16)
        gate_a.append(_dot(xh, wa_ref[n]))
        gate_x.append(_dot(xh, wx_ref[n]))
    r = _sigmoid(jnp.concatenate(gate_a, axis=1) + ba_ref[...])
    i = _sigmoid(jnp.concatenate(gate_x, axis=1) + bx_ref[...])
    log_a = (-LRU_C) * r * _softplus(-lam_ref[...])
    a = jnp.exp(log_a)
    a_s[...] = a
    u_s[...] = jnp.sqrt(1.0 - a * a) * (i * xc)

    def body(g, hp):
        r0 = pl.multiple_of(g * SUBLANES, SUBLANES)
        a8 = a_s[pl.ds(r0, SUBLANES), :]
        u8 = u_s[pl.ds(r0, SUBLANES), :]
        for s in (1, 2, 4):
            ash = jnp.where(row >= s, pltpu.roll(a8, s, 0), 1.0)
            ush = jnp.where(row >= s, pltpu.roll(u8, s, 0), 0.0)
            u8 = a8 * ush + u8
            a8 = a8 * ash
        h8 = a8 * hp + u8
        h_s[pl.ds(r0, SUBLANES), :] = h8
        return h8[SUBLANES - 1:SUBLANES, :]

    hp = lax.fori_loop(0, tt // SUBLANES, body, hprev_ref[0:1, :])
    hprev_ref[0:1, :] = hp
    o_ref[...] = (h_s[...] * jax.nn.gelu(gb_ref[...])).astype(o_ref.dtype)


def _lru_branch(u, conv_w, conv_b, w_a, b_a, w_x, b_x, lam, *, batch, tp):
    m = u.shape[0]
    width = GROUP_WIDTH
    cw = 1024
    tt = _tile(tp, (640, 384, 256, 128))
    nt, nc = tp // tt, width // cw
    nb = cw // LRU_BLOCK_DIM

    def row_blk(b, j, t):
        return b * nt + t

    in_specs = [
        pl.BlockSpec((tt, cw), lambda b, j, t: (row_blk(b, j, t), j)),
        pl.BlockSpec((SUBLANES, cw),
                     lambda b, j, t: (jnp.maximum(row_blk(b, j, t) * (tt // SUBLANES) - 1, 0), j)),
        pl.BlockSpec((tt, cw), lambda b, j, t: (row_blk(b, j, t), nc + j)),
        pl.BlockSpec((CONV_WIDTH, cw), lambda b, j, t: (0, j)),
        pl.BlockSpec((1, cw), lambda b, j, t: (0, j)),
        pl.BlockSpec((nb, LRU_BLOCK_DIM, LRU_BLOCK_DIM), lambda b, j, t: (j, 0, 0)),
        pl.BlockSpec((1, cw), lambda b, j, t: (0, j)),
        pl.BlockSpec((nb, LRU_BLOCK_DIM, LRU_BLOCK_DIM), lambda b, j, t: (j, 0, 0)),
        pl.BlockSpec((1, cw), lambda b, j, t: (0, j)),
        pl.BlockSpec((1, cw), lambda b, j, t: (0, j)),
    ]
    return pl.pallas_call(
        functools.partial(_lru_kernel, tt=tt, cw=cw),
        grid=(batch, nc, nt),
        in_specs=in_specs,
        out_specs=pl.BlockSpec((tt, cw), lambda b, j, t: (row_blk(b, j, t), j)),
        out_shape=jax.ShapeDtypeStruct((m, width), BF16),
        scratch_shapes=[pltpu.VMEM((tt, cw), F32), pltpu.VMEM((tt, cw), F32), pltpu.VMEM((tt, cw), F32),
                        pltpu.VMEM((SUBLANES, cw), F32)],
        compiler_params=pltpu.CompilerParams(
            dimension_semantics=("parallel", "parallel", "arbitrary"),
            vmem_limit_bytes=_vmem_limit(tt * cw * 4 * 12)),
        name="rglru",
    )(u, u, u, conv_w.astype(F32), conv_b.reshape(1, width).astype(F32),
      w_a.astype(BF16), b_a.reshape(1, width).astype(F32),
      w_x.astype(BF16), b_x.reshape(1, width).astype(F32), lam.reshape(1, width).astype(F32))


def _headnorm_kernel(q_ref, k_ref, v_ref, qg_ref, kg_ref, c_ref, s1_ref, s2_ref, qo_ref, ko_ref, vo_ref,
                     *, q_scale):
    rot = (c_ref[...], s1_ref[...], s2_ref[...])
    qo_ref[...] = _head_norm(q_ref[...], qg_ref[...], q_scale, rot, qo_ref.dtype)
    ko_ref[...] = _head_norm(k_ref[...], kg_ref[...], 1.0, rot, ko_ref.dtype)
    vo_ref[...] = v_ref[...].astype(vo_ref.dtype)


def _headnorm_rot(u, q_gain, k_gain, rot_tables, *, q_scale, tp):
    m = u.shape[0]
    width = 512
    ncb = GROUP_WIDTH // width
    tr = _tile(tp, (640, 384, 256, 128))
    nt = tp // tr
    in_specs = [
        pl.BlockSpec((tr, width), lambda i, j: (i, j)),
        pl.BlockSpec((tr, width), lambda i, j: (i, ncb + j)),
        pl.BlockSpec((tr, width), lambda i, j: (i, 2 * ncb + j)),
        pl.BlockSpec((1, LANES), lambda i, j: (0, 0)),
        pl.BlockSpec((1, LANES), lambda i, j: (0, 0)),
    ] + [pl.BlockSpec((tr, LANES), lambda i, j: (i % nt, 0))] * 3
    out = jax.ShapeDtypeStruct((m, GROUP_WIDTH), BF16)
    return pl.pallas_call(
        functools.partial(_headnorm_kernel, q_scale=q_scale),
        grid=(m // tr, ncb),
        in_specs=in_specs,
        out_specs=[pl.BlockSpec((tr, width), lambda i, j: (i, j))] * 3,
        out_shape=[out, out, out],
        compiler_params=pltpu.CompilerParams(
            dimension_semantics=("parallel", "parallel"),
            vmem_limit_bytes=_vmem_limit(tr * width * 24)),
        name="headnorm_rot",
    )(u, u, u, q_gain.reshape(1, LANES).astype(F32), k_gain.reshape(1, LANES).astype(F32), *rot_tables)


def _rotary_tables(tp):
    inv_freq = ROPE_THETA ** (-jnp.arange(ROT_HALF, dtype=F32) / ROT_HALF)
    ang = jnp.arange(tp, dtype=F32)[:, None] * inv_freq[None, :]
    cos, sin = jnp.cos(ang), jnp.sin(ang)
    ones = jnp.ones((tp, LANES - 2 * ROT_HALF), F32)
    zeros = jnp.zeros((tp, LANES - 2 * ROT_HALF), F32)
    zh = jnp.zeros((tp, ROT_HALF), F32)
    c = jnp.concatenate([cos, cos, ones], axis=1)
    s1 = jnp.concatenate([-sin, zh, zeros], axis=1)
    s2 = jnp.concatenate([zh, sin, zeros], axis=1)
    return c, s1, s2


def _softmax_step(s, v, m_ref, l_ref, acc_ref):
    m_prev = m_ref[...]
    m_new = jnp.maximum(m_prev, jnp.max(s, axis=1, keepdims=True))
    alpha = jnp.exp2(m_prev - m_new)
    p = jnp.exp2(s - m_new)
    if l_ref is not None:
        l_ref[...] = alpha * l_ref[...] + jnp.sum(p, axis=1, keepdims=True)
    acc_ref[...] = alpha * acc_ref[...] + _dot(p.astype(BF16), v)
    m_ref[...] = m_new


def _causal_mask(s):
    row = lax.broadcasted_iota(jnp.int32, s.shape, 0)
    col = lax.broadcasted_iota(jnp.int32, s.shape, 1)
    return jnp.where(col <= row, s, MASK_VALUE)


def _kv_sweep(step, qi):
    def body(j, carry):
        for g in range(KV_GROUP):
            step(KV_GROUP * j + g, False)
        return carry

    lax.fori_loop(0, qi // KV_GROUP, body, 0)
    rem = qi % KV_GROUP
    for g in range(1, KV_GROUP):
        @pl.when(rem >= g)
        def _(g=g):
            step(qi - rem + g - 1, False)

    step(qi, True)


def _diff_flash_kernel(q1_ref, q2_ref, k1_ref, k2_ref, v_ref, lq1_ref, lk1_ref, lq2_ref, lk2_ref, sg_ref,
                       o_ref, m1, l1, acc1, m2, l2, acc2, *, lambda_init, blk):
    qi = pl.program_id(2)
    sets = ((q1_ref, k1_ref, m1, l1, acc1), (q2_ref, k2_ref, m2, l2, acc2))
    for _, _, m, l, acc in sets:
        m[...] = jnp.full_like(m, MASK_VALUE)
        l[...] = jnp.zeros_like(l)
        acc[...] = jnp.zeros_like(acc)

    def step(j, masked):
        r0 = pl.multiple_of(j * blk, blk)
        v = v_ref[pl.ds(r0, blk), :]
        for q_ref, k_ref, m, l, acc in sets:
            s = _nt(q_ref[...], k_ref[pl.ds(r0, blk), :])
            _softmax_step(_causal_mask(s) if masked else s, v, m, l, acc)

    _kv_sweep(step, qi)
    lam = (jnp.exp(jnp.sum(lq1_ref[...] * lk1_ref[...], axis=1, keepdims=True))
           - jnp.exp(jnp.sum(lq2_ref[...] * lk2_ref[...], axis=1, keepdims=True)) + lambda_init)
    o = acc1[...] / l1[...] - lam * (acc2[...] / l2[...])
    o = o * lax.rsqrt(jnp.mean(o * o, axis=-1, keepdims=True) + NORM_EPS)
    o_ref[...] = (o * sg_ref[...] * (1.0 - lambda_init)).astype(o_ref.dtype)


def _diff_attention(qn, kn, vb, lq1, lk1, lq2, lk2, sub_gain, *, lambda_init, batch, tp):
    m = qn.shape[0]
    blk = _tile(tp, (640, 384, 256, 128))
    nq = tp // blk
    hd, vd = DIFF_HEAD_DIM, 2 * DIFF_HEAD_DIM
    vec = lambda p: p.reshape(1, -1).astype(F32)
    pspec = lambda w: pl.BlockSpec((1, w), lambda b, h, i: (0, 0))
    return pl.pallas_call(
        functools.partial(_diff_flash_kernel, lambda_init=lambda_init, blk=blk),
        grid=(batch, DIFF_HEADS, nq),
        in_specs=[
            pl.BlockSpec((blk, hd), lambda b, h, i: (b * nq + i, 2 * h)),
            pl.BlockSpec((blk, hd), lambda b, h, i: (b * nq + i, 2 * h + 1)),
            pl.BlockSpec((tp, hd), lambda b, h, i: (b, 2 * h)),
            pl.BlockSpec((tp, hd), lambda b, h, i: (b, 2 * h + 1)),
            pl.BlockSpec((tp, vd), lambda b, h, i: (b, h)),
            pspec(hd), pspec(hd), pspec(hd), pspec(hd), pspec(vd),
        ],
        out_specs=pl.BlockSpec((blk, vd), lambda b, h, i: (b * nq + i, h)),
        out_shape=jax.ShapeDtypeStruct((m, GROUP_WIDTH), BF16),
        scratch_shapes=[pltpu.VMEM((blk, 1), F32), pltpu.VMEM((blk, 1), F32), pltpu.VMEM((blk, vd), F32),
                        pltpu.VMEM((blk, 1), F32), pltpu.VMEM((blk, 1), F32), pltpu.VMEM((blk, vd), F32)],
        compiler_params=pltpu.CompilerParams(
            dimension_semantics=("parallel", "parallel", "arbitrary"),
            vmem_limit_bytes=_vmem_limit(tp * (2 * hd + vd) * 2 + blk * blk * 4 * 6)),
        name="diff_attention",
    )(qn, qn, kn, kn, vb, vec(lq1), vec(lk1), vec(lq2), vec(lk2), vec(sub_gain))


def _fox_flash_kernel(q_ref, k_ref, v_ref, c_ref, o_ref, vaug_ref, m_ref, acc_ref, *, blk):
    qi = pl.program_id(2)
    hd = FOX_HEAD_DIM

    @pl.when(qi == 0)
    def _():
        vaug_ref[:, 0:hd] = v_ref[...]
        vaug_ref[:, hd:2 * hd] = jnp.ones((vaug_ref.shape[0], hd), vaug_ref.dtype)

    m_ref[...] = jnp.full_like(m_ref, MASK_VALUE)
    acc_ref[...] = jnp.zeros_like(acc_ref)
    cq = c_ref[0, qi][:, 0:1]

    def step(j, masked):
        r0 = pl.multiple_of(j * blk, blk)
        s = _nt(q_ref[...], k_ref[pl.ds(r0, blk), :]) + (cq - c_ref[0, j]) * LOG2E
        _softmax_step(_causal_mask(s) if masked else s, vaug_ref[pl.ds(r0, blk), :], m_ref, None, acc_ref)

    _kv_sweep(step, qi)
    acc = acc_ref[...]
    o_ref[...] = (acc[:, 0:hd] / acc[:, hd:2 * hd]).astype(o_ref.dtype)


def _fox_attention(qn, kn, vb, c_rows, *, batch, tp):
    m = qn.shape[0]
    blk = _tile(tp, (640, 384, 256, 128))
    nq = tp // blk
    hd = FOX_HEAD_DIM
    c_blocks = c_rows.reshape(batch * FOX_HEADS, nq, 1, blk)
    return pl.pallas_call(
        functools.partial(_fox_flash_kernel, blk=blk),
        grid=(batch, FOX_HEADS, nq),
        in_specs=[
            pl.BlockSpec((blk, hd), lambda b, h, i: (b * nq + i, h)),
            pl.BlockSpec((tp, hd), lambda b, h, i: (b, h)),
            pl.BlockSpec((tp, hd), lambda b, h, i: (b, h)),
            pl.BlockSpec((1, nq, 1, blk), lambda b, h, i: (b * FOX_HEADS + h, 0, 0, 0)),
        ],
        out_specs=pl.BlockSpec((blk, hd), lambda b, h, i: (b * nq + i, h)),
        out_shape=jax.ShapeDtypeStruct((m, GROUP_WIDTH), BF16),
        scratch_shapes=[pltpu.VMEM((tp, 2 * hd), BF16), pltpu.VMEM((blk, 1), F32),
                        pltpu.VMEM((blk, 2 * hd), F32)],
        compiler_params=pltpu.CompilerParams(
            dimension_semantics=("parallel", "parallel", "arbitrary"),
            vmem_limit_bytes=_vmem_limit(tp * hd * 2 * 4 + blk * blk * 4 * 6)),
        name="fox_attention",
    )(qn, kn, vb, c_blocks)


def _fox_gate_kernel(hn_ref, wf_ref, bias_ref, c_ref, carry_ref, *, tc):
    t = pl.program_id(1)

    @pl.when(t == 0)
    def _():
        carry_ref[...] = jnp.zeros_like(carry_ref)

    f = _nt(wf_ref[...], hn_ref[...]) + bias_ref[...]
    log_f = jnp.minimum(f, 0.0) - jnp.log1p(jnp.exp(-jnp.abs(f)))
    upper = (lax.broadcasted_iota(jnp.int32, (tc, tc), 0)
             <= lax.broadcasted_iota(jnp.int32, (tc, tc), 1)).astype(BF16)
    hi, mid, lo = _split3(log_f)
    c = _dot(hi, upper) + _dot(mid, upper) + _dot(lo, upper) + carry_ref[:, 0:1]
    c_ref[0] = c
    carry_ref[...] = jnp.broadcast_to(c[:, tc - 1:tc], carry_ref.shape)


def _fox_gates(hn, wf_t, f_bias, *, batch, tp):
    d = hn.shape[1]
    tc = _tile(tp, (640, 384, 256, 128))
    nt = tp // tc
    return pl.pallas_call(
        functools.partial(_fox_gate_kernel, tc=tc),
        grid=(batch, nt),
        in_specs=[pl.BlockSpec((tc, d), lambda b, t: (b * nt + t, 0)),
                  pl.BlockSpec((FOX_HEADS, d), lambda b, t: (0, 0)),
                  pl.BlockSpec((FOX_HEADS, 1), lambda b, t: (0, 0))],
        out_specs=pl.BlockSpec((1, FOX_HEADS, tc), lambda b, t: (b, 0, t)),
        out_shape=jax.ShapeDtypeStruct((batch, FOX_HEADS, tp), F32),
        scratch_shapes=[pltpu.VMEM((FOX_HEADS, LANES), F32)],
        compiler_params=pltpu.CompilerParams(
            dimension_semantics=("parallel", "arbitrary"),
            vmem_limit_bytes=_vmem_limit(tc * d * 2 + tc * tc * 8)),
        name="fox_gates",
    )(hn, wf_t, f_bias.reshape(FOX_HEADS, 1).astype(F32))


def _head_sum(x, ones_bd):
    cols = []
    for g in range(x.shape[1] // LANES):
        hi, lo = _split2(x[:, g * LANES:(g + 1) * LANES])
        cols.append(_dot(hi, ones_bd) + _dot(lo, ones_bd))
    return jnp.concatenate(cols, axis=1)


def _head_ones():
    r = lax.broadcasted_iota(jnp.int32, (LANES, LANES), 0) < RWKV_HEAD_DIM
    c = lax.broadcasted_iota(jnp.int32, (LANES, LANES), 1) < RWKV_HEAD_DIM
    return jnp.where(r == c, 1.0, 0.0).astype(BF16)


def _rwkv_pre_kernel(zr_ref, zrh_ref, zk_ref, zkh_ref, zv_ref, zvh_ref, zl_ref, zlh_ref,
                     mur_ref, muk_ref, muv_ref, mul_ref, w0_ref, wup_ref, a0_ref, aup_ref, gup_ref,
                     kk_ref, ka_ref,
                     r_o, k_o, v_o, a_o, b_o, lw_o, g_o, lora_ref, *, tr, nt):
    first = (pl.program_id(0) % nt) == 0

    def shift(x_ref, h_ref, mu_ref):
        x = x_ref[...]
        rolled = pltpu.roll(x, 1, 0)
        before = jnp.where(first, 0.0, h_ref[SUBLANES - 1:SUBLANES, :])
        row = lax.broadcasted_iota(jnp.int32, (SUBLANES, x.shape[1]), 0)
        top = jnp.where(row == 0, before, rolled[0:SUBLANES, :])
        prev = jnp.concatenate([top, rolled[SUBLANES:, :]], axis=0)
        return x + (prev - x) * mu_ref[...]

    n1, n2 = DECAY_LORA, DECAY_LORA + AAA_LORA

    @pl.when(pl.program_id(1) == 0)
    def _():
        zl = shift(zl_ref, zlh_ref, mul_ref)
        lora_ref[:, 0:n1] = jnp.tanh(zl[:, 0:n1]).astype(BF16)
        lora_ref[:, n1:n2] = zl[:, n1:n2].astype(BF16)
        lora_ref[:, n2:] = _sigmoid(zl[:, n2:]).astype(BF16)

    r = shift(zr_ref, zrh_ref, mur_ref)
    k = shift(zk_ref, zkh_ref, muk_ref)
    v = shift(zv_ref, zvh_ref, muv_ref)
    lw = (-math.exp(-0.5)) * _sigmoid(w0_ref[...] + _dot(lora_ref[:, 0:n1], wup_ref[...]))
    a = _sigmoid(a0_ref[...] + _dot(lora_ref[:, n1:n2], aup_ref[...]))
    g = _dot(lora_ref[:, n2:], gup_ref[...])
    kx = k * kk_ref[...]
    kk = kx * lax.rsqrt(jnp.maximum(_head_sum(kx * kx, _head_ones()), 1e-24))
    r_o[...] = r.astype(r_o.dtype)
    k_o[...] = (k * (1.0 + (a - 1.0) * ka_ref[...])).astype(k_o.dtype)
    v_o[...] = v.astype(v_o.dtype)
    a_o[...] = (-kk).astype(a_o.dtype)
    b_o[...] = (kk * a).astype(b_o.dtype)
    lw_o[...] = lw
    g_o[...] = g.astype(g_o.dtype)


def _rwkv_pre(z, mu, w0, w_up, a0, a_up, g_up, k_k, k_a, *, tp):
    m = z.shape[0]
    width = GROUP_WIDTH
    cw = 512
    ncb = width // cw
    lw = RWKV_LORA_PAD
    tr = _tile(tp, (320, 384, 256, 128))
    nt = tp // tr
    hb = tr // SUBLANES

    def halo(i):
        return jnp.maximum(i * hb - 1, 0)

    mu2 = mu.reshape(1, -1).astype(F32)
    row = lambda p: p.reshape(1, width).astype(F32)
    g_up_p = jnp.zeros((lw - DECAY_LORA - AAA_LORA, width), BF16).at[:GATE_LORA].set(g_up.astype(BF16))
    in_specs = []
    for c in range(3):
        in_specs.append(pl.BlockSpec((tr, cw), lambda i, j, c=c: (i, c * ncb + j)))
        in_specs.append(pl.BlockSpec((SUBLANES, cw), lambda i, j, c=c: (halo(i), c * ncb + j)))
    in_specs.append(pl.BlockSpec((tr, lw), lambda i, j: (i, 3 * width // lw)))
    in_specs.append(pl.BlockSpec((SUBLANES, lw), lambda i, j: (halo(i), 3 * width // lw)))
    for c in range(3):
        in_specs.append(pl.BlockSpec((1, cw), lambda i, j, c=c: (0, c * ncb + j)))
    in_specs.append(pl.BlockSpec((1, lw), lambda i, j: (0, 3 * width // lw)))
    in_specs += [
        pl.BlockSpec((1, cw), lambda i, j: (0, j)),
        pl.BlockSpec((DECAY_LORA, cw), lambda i, j: (0, j)),
        pl.BlockSpec((1, cw), lambda i, j: (0, j)),
        pl.BlockSpec((AAA_LORA, cw), lambda i, j: (0, j)),
        pl.BlockSpec((lw - DECAY_LORA - AAA_LORA, cw), lambda i, j: (0, j)),
        pl.BlockSpec((1, cw), lambda i, j: (0, j)),
        pl.BlockSpec((1, cw), lambda i, j: (0, j)),
    ]
    out16 = jax.ShapeDtypeStruct((m, width), BF16)
    out32 = jax.ShapeDtypeStruct((m, width), F32)
    return pl.pallas_call(
        functools.partial(_rwkv_pre_kernel, tr=tr, nt=nt),
        grid=(m // tr, ncb),
        in_specs=in_specs,
        out_specs=[pl.BlockSpec((tr, cw), lambda i, j: (i, j))] * 7,
        out_shape=[out16] * 5 + [out32, out16],
        scratch_shapes=[pltpu.VMEM((tr, lw), BF16)],
        compiler_params=pltpu.CompilerParams(
            dimension_semantics=("parallel", "arbitrary"),
            vmem_limit_bytes=_vmem_limit(tr * cw * 4 * 16 + tr * lw * 4 * 3)),
        name="rwkv_pre",
    )(z, z, z, z, z, z, z, z, mu2, mu2, mu2, mu2, row(w0), w_up.astype(BF16), row(a0),
      a_up.astype(BF16), g_up_p, row(k_k), row(k_a))


def _apply_and_square(x, r, square, precise):
    n = x.shape[1]
    xh, xl = _split2(x)
    rh, rl = _split2(r)
    if square:
        rh = jnp.concatenate([xh, rh], axis=1)
        rl = jnp.concatenate([xl, rl], axis=1)
    p = _dot(xh, rh)
    if precise:
        p = p + _dot(xh, rl) + _dot(xl, rh)
    if square:
        return p[:, 0:n], r + p[:, n:]
    return x, r + p


def _rwkv_scan_kernel(r_ref, k_ref, v_ref, a_ref, b_ref, lw_ref, y_ref, s_ref, *, tt, pairs):
    C = RWKV_CHUNK
    t = pl.program_id(2)

    @pl.when(t == 0)
    def _():
        s_ref[...] = jnp.zeros_like(s_ref)

    lane_top = lax.broadcasted_iota(jnp.int32, (C, LANES), 1) < RWKV_HEAD_DIM
    ri = lax.broadcasted_iota(jnp.int32, (2 * C, 2 * C), 0)
    ci = lax.broadcasted_iota(jnp.int32, (2 * C, 2 * C), 1)
    strict = (ci & (C - 1)) < (ri & (C - 1))
    incl = (ci & (C - 1)) <= (ri & (C - 1))
    strict2 = jnp.concatenate([strict, strict], axis=1)
    incl2 = jnp.concatenate([incl, incl], axis=1)
    lower = (lax.broadcasted_iota(jnp.int32, (C, C), 1)
             <= lax.broadcasted_iota(jnp.int32, (C, C), 0)).astype(BF16)

    def stack(x):
        return jnp.concatenate([jnp.where(lane_top, x, 0.0), jnp.where(lane_top, 0.0, x)], axis=0)

    def chunk(c, carry):
        r0 = pl.multiple_of(c * C, C)
        P = range(pairs)
        sls = [slice(p * LANES, (p + 1) * LANES) for p in P]
        lw = [lw_ref[pl.ds(r0, C), sl] for sl in sls]
        sp = [_split3(x) for x in lw]
        g_in = [_dot(lower, hi) + _dot(lower, mid) + _dot(lower, lo) for hi, mid, lo in sp]
        g_last = [g[C - 1:C, :] for g in g_in]
        e_in = [jnp.exp(g) for g in g_in]
        e_ex = [jnp.exp(g - l) for g, l in zip(g_in, lw)]
        e_neg = [jnp.exp(-g) for g in g_in]
        e_rem = [jnp.exp(gl - g) for g, gl in zip(g_in, g_last)]
        r = [r_ref[pl.ds(r0, C), sl].astype(F32) for sl in sls]
        k = [k_ref[pl.ds(r0, C), sl].astype(F32) for sl in sls]
        a = [a_ref[pl.ds(r0, C), sl].astype(F32) for sl in sls]
        b = [b_ref[pl.ds(r0, C), sl].astype(F32) for sl in sls]
        xa = [stack(a[p] * e_ex[p]).astype(BF16) for p in P]
        xr = [stack(r[p] * e_in[p]).astype(BF16) for p in P]
        ybk = [jnp.concatenate([stack(b[p] * e_neg[p]), stack(k[p] * e_neg[p])], axis=0).astype(BF16) for p in P]
        vs = [stack(v_ref[pl.ds(r0, C), sl].astype(F32)).astype(BF16) for sl in sls]
        bk = [jnp.concatenate([stack(b[p] * e_rem[p]), stack(k[p] * e_rem[p])], axis=0).astype(BF16) for p in P]
        lp = [jnp.where(strict2, _nt(xa[p], ybk[p]), 0.0) for p in P]
        mp = [jnp.where(incl2, _nt(xr[p], ybk[p]), 0.0).astype(BF16) for p in P]
        s0 = [s_ref[p] for p in P]
        s0b = [s.astype(BF16) for s in s0]
        x = [l[:, 0:2 * C] for l in lp]
        u = [_nt(xa[p], s0b[p]) + _dot(lp[p][:, 2 * C:].astype(BF16), vs[p]) for p in P]
        levels = int(math.log2(C))
        for lvl in range(levels):
            x, u = zip(*[_apply_and_square(xx, uu, lvl < levels - 1, lvl < RWKV_PRECISE_LEVELS)
                         for xx, uu in zip(x, u)])
        uv = [jnp.concatenate([u[p].astype(BF16), vs[p]], axis=0) for p in P]
        ys = [_nt(xr[p], s0b[p]) + _dot(mp[p], uv[p]) for p in P]
        for p in P:
            y_ref[pl.ds(r0, C), sls[p]] = ys[p][0:C, :] + ys[p][C:2 * C, :]
        for p in P:
            s_ref[p] = s0[p] * jnp.exp(g_last[p]) + _tn(uv[p], bk[p])
        return carry

    lax.fori_loop(0, tt // C, chunk, 0)


def _rwkv_scan(r, k, v, a, b, lw, *, batch, tp):
    m = r.shape[0]
    pairs = 8
    cw = pairs * LANES
    tt = _tile(tp, (640, 384, 256, 128))
    nt, nc = tp // tt, GROUP_WIDTH // cw
    spec = pl.BlockSpec((tt, cw), lambda bi, j, t: (bi * nt + t, j))
    return pl.pallas_call(
        functools.partial(_rwkv_scan_kernel, tt=tt, pairs=pairs),
        grid=(batch, nc, nt),
        in_specs=[spec] * 6,
        out_specs=spec,
        out_shape=jax.ShapeDtypeStruct((m, GROUP_WIDTH), F32),
        scratch_shapes=[pltpu.VMEM((pairs, LANES, LANES), F32)],
        compiler_params=pltpu.CompilerParams(
            dimension_semantics=("parallel", "parallel", "arbitrary"),
            vmem_limit_bytes=_vmem_limit(tt * cw * 4 * 8)),
        name="rwkv_scan",
    )(r, k, v, a, b, lw)


def _rwkv_post_kernel(y_ref, r_ref, k_ref, v_ref, g_ref, rk_ref, gw_ref, gb_ref, o_ref):
    ones_bd = _head_ones()
    inv_n = 1.0 / RWKV_HEAD_DIM
    y = y_ref[...]
    mean = _head_sum(y, ones_bd) * inv_n
    d = y - mean
    var = _head_sum(d * d, ones_bd) * inv_n
    yn = d * lax.rsqrt(var + RWKV_GN_EPS) * gw_ref[...] + gb_ref[...]
    rk = r_ref[...].astype(F32) * k_ref[...].astype(F32) * rk_ref[...]
    bonus = _head_sum(rk, ones_bd) * v_ref[...].astype(F32)
    o_ref[...] = ((yn + bonus) * g_ref[...]).astype(o_ref.dtype)


def _rwkv_post(y, r, k, v, g, r_k, gn_w, gn_b, *, tp):
    m = y.shape[0]
    cw = 512
    tr = _tile(tp, (640, 384, 256, 128))
    spec = pl.BlockSpec((tr, cw), lambda i, j: (i, j))
    pspec = pl.BlockSpec((1, cw), lambda i, j: (0, j))
    row = lambda p: p.reshape(1, GROUP_WIDTH).astype(F32)
    return pl.pallas_call(
        _rwkv_post_kernel,
        grid=(m // tr, GROUP_WIDTH // cw),
        in_specs=[spec] * 5 + [pspec] * 3,
        out_specs=spec,
        out_shape=jax.ShapeDtypeStruct((m, GROUP_WIDTH), BF16),
        compiler_params=pltpu.CompilerParams(
            dimension_semantics=("parallel", "parallel"),
            vmem_limit_bytes=_vmem_limit(tr * cw * 4 * 10)),
        name="rwkv_post",
    )(y, r, k, v, g, row(r_k), row(gn_w), row(gn_b))


def _mlp(h, f_gain, f_up, f_down, tm):
    hn = _rmsnorm_rows(h, f_gain)
    hid = _matmul([hn], f_up, n_off=0, n=D_FF, tm=tm, tn=512, out_dtype=BF16, relu2=True, name="ffn_up")
    return _matmul([hid], f_down.astype(BF16), n_off=0, n=D_MODEL, tm=tm, tn=1024, tk=2048,
                   residual=h, name="ffn_down")


def kernel(x, meta_tokens, mix_norm_0, w_in_0, conv_w_0, conv_b_0, lru_wa_0, lru_ba_0, lru_wx_0, lru_bx_0, lru_lam_0, diff_q_gain_0, diff_k_gain_0, diff_lq1_0, diff_lk1_0, diff_lq2_0, diff_lk2_0, diff_sub_gain_0, w_out_0, ffn_norm_0, ffn_up_0, ffn_down_0, mix_norm_1, w_in_1, rwkv_mu_1, rwkv_w0_1, rwkv_w_up_1, rwkv_a0_1, rwkv_a_up_1, rwkv_g_up_1, rwkv_k_k_1, rwkv_k_a_1, rwkv_r_k_1, rwkv_gn_w_1, rwkv_gn_b_1, fox_q_gain_1, fox_k_gain_1, fox_f_bias_1, w_out_1, ffn_norm_1, ffn_up_1, ffn_down_1):
    batch, seq, d = x.shape
    assert d == D_MODEL
    t_real = seq + N_META
    tp = -(-t_real // LANES) * LANES
    m = batch * tp
    tm = _tile(m, (1280, 768, 640, 512, 384, 256, 128))
    gw = GROUP_WIDTH

    meta = jnp.broadcast_to(meta_tokens[None].astype(x.dtype), (batch, N_META, d))
    pad = jnp.zeros((batch, tp - t_real, d), x.dtype)
    h = jnp.concatenate([meta, x, pad], axis=1).reshape(m, d)

    hn = _rmsnorm_rows(h, mix_norm_0)
    u_lru = _matmul([hn], w_in_0, n_off=0, n=2 * gw, tm=tm, tn=512, name="in_proj0_lru")
    u_att = _matmul([hn], w_in_0, n_off=2 * gw, n=3 * gw, tm=tm, tn=512, name="in_proj0_att")
    qn, kn, vb = _headnorm_rot(u_att, diff_q_gain_0, diff_k_gain_0, _rotary_tables(tp),
                               q_scale=DIFF_HEAD_DIM ** -0.5 * LOG2E, tp=tp)
    y_a = _lru_branch(u_lru, conv_w_0, conv_b_0, lru_wa_0, lru_ba_0, lru_wx_0, lru_bx_0, lru_lam_0,
                      batch=batch, tp=tp)
    lambda_init = 0.8 - 0.6 * math.exp(-0.3 * 0)
    y_b = _diff_attention(qn, kn, vb, diff_lq1_0, diff_lk1_0, diff_lq2_0, diff_lk2_0, diff_sub_gain_0,
                          lambda_init=lambda_init, batch=batch, tp=tp)
    h = _matmul([y_a, y_b], w_out_0, n_off=0, n=d, tm=tm, tn=512, residual=h, name="out_proj0")
    h = _mlp(h, ffn_norm_0, ffn_up_0, ffn_down_0, tm)

    hn = _rmsnorm_rows(h, mix_norm_1)
    slab = 3 * gw + DECAY_LORA + AAA_LORA + GATE_LORA
    zw = 3 * gw + RWKV_LORA_PAD
    w_z = jnp.zeros((d, zw), BF16).at[:, :slab].set(w_in_1[:, :slab].astype(BF16))
    w_f = w_in_1[:, slab:slab + 3 * gw].astype(BF16)
    wf_t = w_in_1[:, slab + 3 * gw:].T.astype(BF16)
    mu = jnp.zeros((zw,), F32).at[:slab].set(rwkv_mu_1.astype(F32))
    z = _matmul([hn], w_z, n_off=0, n=zw, tm=tm, tn=512, name="in_proj1_rwkv")
    fq = _matmul([hn], w_f, n_off=0, n=gw, tm=tm, tn=512, out_dtype=BF16,
                 head_norm=(fox_q_gain_1, FOX_HEAD_DIM ** -0.5 * LOG2E), name="in_proj1_fq")
    fk = _matmul([hn], w_f, n_off=gw, n=gw, tm=tm, tn=512, out_dtype=BF16,
                 head_norm=(fox_k_gain_1, 1.0), name="in_proj1_fk")
    fv = _matmul([hn], w_f, n_off=2 * gw, n=gw, tm=tm, tn=512, out_dtype=BF16, name="in_proj1_fv")
    c_rows = _fox_gates(hn, wf_t, fox_f_bias_1, batch=batch, tp=tp)

    r, k32, v, a_neg, b_vec, log_w, g = _rwkv_pre(z, mu, rwkv_w0_1, rwkv_w_up_1, rwkv_a0_1, rwkv_a_up_1,
                                                  rwkv_g_up_1, rwkv_k_k_1, rwkv_k_a_1, tp=tp)
    y = _rwkv_scan(r, k32, v, a_neg, b_vec, log_w, batch=batch, tp=tp)
    y_c = _rwkv_post(y, r, k32, v, g, rwkv_r_k_1, rwkv_gn_w_1, rwkv_gn_b_1, tp=tp)
    y_d = _fox_attention(fq, fk, fv, c_rows, batch=batch, tp=tp)
    h = _matmul([y_c, y_d], w_out_1, n_off=0, n=d, tm=tm, tn=512, residual=h, name="out_proj1")
    h = _mlp(h, ffn_norm_1, ffn_up_1, ffn_down_1, tm)

    return h.reshape(batch, tp, d)[:, N_META:N_META + seq]
```

```python
import functools
import math

import jax
import jax.numpy as jnp
from jax import lax
from jax.experimental import pallas as pl
from jax.experimental.pallas import tpu as pltpu

F32 = jnp.float32
BF16 = jnp.bfloat16

D_MODEL = 4096
N_META = 16
GROUP_WIDTH = D_MODEL // 2
D_FF = 4 * D_MODEL
NORM_EPS = 1e-6
ROPE_THETA = 500000.0

LRU_BLOCK_DIM = 128
CONV_WIDTH = 4
LRU_C = 8.0

DIFF_HEAD_DIM = 128
DIFF_HEADS = GROUP_WIDTH // (2 * DIFF_HEAD_DIM)
ROT_HALF = DIFF_HEAD_DIM // 8

RWKV_HEAD_DIM = 64
RWKV_CHUNK = 64
DECAY_LORA = 128
AAA_LORA = 128
GATE_LORA = 480
RWKV_LORA_PAD = 1024
RWKV_GN_EPS = 64e-5
RWKV_PRECISE_LEVELS = 2

FOX_HEAD_DIM = 128
FOX_HEADS = GROUP_WIDTH // FOX_HEAD_DIM

LANES = 128
SUBLANES = 8
VMEM_CAP_BYTES = 56 * 1024 * 1024
MASK_VALUE = -1e30
LOG2E = math.log2(math.e)
KV_PAIR = 2
DIFF_ROW_CHUNK = 320
FOX_ROW_CHUNK = 128


def _tile(n, prefs):
    for p in prefs:
        if n % p == 0:
            return p
    raise ValueError(f"no tile of {prefs} divides {n}")


def _vmem_limit(block_bytes):
    return int(min(VMEM_CAP_BYTES, 2 * block_bytes + (8 << 20)))


def _nt(a, b):
    return lax.dot_general(a, b, (((1,), (1,)), ((), ())), preferred_element_type=F32)


def _tn(a, b):
    return lax.dot_general(a, b, (((0,), (0,)), ((), ())), preferred_element_type=F32)


def _dot(a, b):
    return jnp.dot(a, b, preferred_element_type=F32)


def _split2(x):
    hi = x.astype(BF16)
    lo = (x - hi.astype(F32)).astype(BF16)
    return hi, lo


def _split3(x):
    hi = x.astype(BF16)
    r1 = x - hi.astype(F32)
    mid = r1.astype(BF16)
    lo = (r1 - mid.astype(F32)).astype(BF16)
    return hi, mid, lo


def _sigmoid(x):
    return 0.5 * jnp.tanh(0.5 * x) + 0.5


def _softplus(x):
    return jnp.maximum(x, 0.0) + jnp.log1p(jnp.exp(-jnp.abs(x)))


def _rmsnorm_kernel(x_ref, g_ref, o_ref):
    x = x_ref[...]
    y = x * lax.rsqrt(jnp.mean(x * x, axis=-1, keepdims=True) + NORM_EPS)
    o_ref[...] = (y * g_ref[...]).astype(o_ref.dtype)


def _rmsnorm_rows(x, gain):
    m, d = x.shape
    tr = _tile(m, (320, 256, 128))
    return pl.pallas_call(
        _rmsnorm_kernel,
        grid=(m // tr,),
        in_specs=[pl.BlockSpec((tr, d), lambda i: (i, 0)),
                  pl.BlockSpec((1, d), lambda i: (0, 0))],
        out_specs=pl.BlockSpec((tr, d), lambda i: (i, 0)),
        out_shape=jax.ShapeDtypeStruct((m, d), BF16),
        compiler_params=pltpu.CompilerParams(
            dimension_semantics=("parallel",),
            vmem_limit_bytes=_vmem_limit(tr * d * 6)),
        name="rmsnorm_rows",
    )(x, gain.reshape(1, d).astype(F32))


def _head_norm(x, gain, scale, rot, out_dtype):
    cols = []
    for g in range(x.shape[1] // LANES):
        xs = x[:, g * LANES:(g + 1) * LANES]
        y = xs * lax.rsqrt(jnp.mean(xs * xs, axis=-1, keepdims=True) + NORM_EPS) * gain
        if rot is not None:
            c, s1, s2 = rot
            y = y * c + pltpu.roll(y, LANES - ROT_HALF, 1) * s1 + pltpu.roll(y, ROT_HALF, 1) * s2
        if scale != 1.0:
            y = y * scale
        cols.append(y.astype(out_dtype))
    return jnp.concatenate(cols, axis=1)


def _mm_kernel(*refs, n_a, k_sizes, nk, has_res, relu2, norm_scale):
    a_refs = refs[:n_a]
    w_ref = refs[n_a]
    pos = n_a + 1
    res_ref = gain_ref = None
    if has_res:
        res_ref = refs[pos]
        pos += 1
    if norm_scale is not None:
        gain_ref = refs[pos]
        pos += 1
    o_ref = refs[pos]
    acc_ref = refs[pos + 1] if nk > 1 else None

    def partial_product():
        acc = None
        off = 0
        for a_ref, ks in zip(a_refs, k_sizes):
            p = _dot(a_ref[...], w_ref[off:off + ks, :].astype(BF16))
            acc = p if acc is None else acc + p
            off += ks
        return acc

    def finish(acc):
        if relu2:
            acc = jnp.square(jnp.maximum(acc, 0.0))
        if has_res:
            acc = acc + res_ref[...]
        if norm_scale is not None:
            o_ref[...] = _head_norm(acc, gain_ref[...], norm_scale, None, o_ref.dtype)
        else:
            o_ref[...] = acc.astype(o_ref.dtype)

    if nk == 1:
        finish(partial_product())
        return

    k = pl.program_id(2)

    @pl.when(k == 0)
    def _():
        acc_ref[...] = jnp.zeros_like(acc_ref)

    acc_ref[...] += partial_product()

    @pl.when(k == nk - 1)
    def _():
        finish(acc_ref[...])


def _matmul(a_list, w, *, n_off, n, tm, tn, tk=None, out_dtype=F32, residual=None, relu2=False,
            head_norm=None, name):
    m = a_list[0].shape[0]
    k_sizes = tuple(a.shape[1] for a in a_list)
    k_total = sum(k_sizes)
    if tk is None:
        tk = k_total
    assert len(a_list) == 1 or tk == k_total
    assert m % tm == 0 and n % tn == 0 and n_off % tn == 0 and k_total % tk == 0
    nk = k_total // tk
    j_off = n_off // tn
    blk_k = (tk,) if len(a_list) == 1 else k_sizes
    in_specs = [pl.BlockSpec((tm, bk), lambda i, j, k: (i, k)) for bk in blk_k]
    in_specs.append(pl.BlockSpec((tk, tn), lambda i, j, k: (k, j + j_off)))
    args = list(a_list) + [w]
    if residual is not None:
        in_specs.append(pl.BlockSpec((tm, tn), lambda i, j, k: (i, j)))
        args.append(residual)
    norm_scale = None
    if head_norm is not None:
        gain, norm_scale = head_norm
        in_specs.append(pl.BlockSpec((1, LANES), lambda i, j, k: (0, 0)))
        args.append(gain.reshape(1, LANES).astype(F32))
    out_bytes = jnp.dtype(out_dtype).itemsize
    block_bytes = (tm * tk * 2 + tk * tn * w.dtype.itemsize + tm * tn * out_bytes
                   + (tm * tn * 4 if residual is not None else 0))
    scratch = [pltpu.VMEM((tm, tn), F32)] if nk > 1 else []
    return pl.pallas_call(
        functools.partial(_mm_kernel, n_a=len(a_list), k_sizes=blk_k, nk=nk,
                          has_res=residual is not None, relu2=relu2, norm_scale=norm_scale),
        grid=(m // tm, n // tn, nk),
        in_specs=in_specs,
        out_specs=pl.BlockSpec((tm, tn), lambda i, j, k: (i, j)),
        out_shape=jax.ShapeDtypeStruct((m, n), out_dtype),
        scratch_shapes=scratch,
        compiler_params=pltpu.CompilerParams(
            dimension_semantics=("parallel", "parallel", "arbitrary"),
            vmem_limit_bytes=_vmem_limit(block_bytes + tm * tn * 2)),
        name=name,
    )(*args)


def _lru_kernel(xb_ref, xh_ref, gb_ref, cw_ref, cb_ref, wa_ref, ba_ref, wx_ref, bx_ref, lam_ref,
                o_ref, a_s, u_s, h_s, hprev_ref, *, tt, cw):
    t = pl.program_id(2)

    @pl.when(t == 0)
    def _():
        hprev_ref[...] = jnp.zeros_like(hprev_ref)

    x = xb_ref[...]
    halo = jnp.where(t == 0, 0.0, xh_ref[...])
    w = cw_ref[...]
    row = lax.broadcasted_iota(jnp.int32, (SUBLANES, cw), 0)
    xc = cb_ref[...] + x * w[CONV_WIDTH - 1:CONV_WIDTH, :]
    for s in range(1, CONV_WIDTH):
        rolled = pltpu.roll(x, s, 0)
        top = jnp.where(row < s, pltpu.roll(halo, s, 0), rolled[0:SUBLANES, :])
        shifted = jnp.concatenate([top, rolled[SUBLANES:, :]], axis=0)
        xc = xc + shifted * w[CONV_WIDTH - 1 - s:CONV_WIDTH - s, :]

    gate_a, gate_x = [], []
    for n in range(cw // LRU_BLOCK_DIM):
        xh = xc[:, n * LRU_BLOCK_DIM:(n + 1) * LRU_BLOCK_DIM].astype(BF16)
        gate_a.append(_dot(xh, wa_ref[n]))
        gate_x.append(_dot(xh, wx_ref[n]))
    r = _sigmoid(jnp.concatenate(gate_a, axis=1) + ba_ref[...])
    i = _sigmoid(jnp.concatenate(gate_x, axis=1) + bx_ref[...])
    log_a = (-LRU_C) * r * _softplus(-lam_ref[...])
    a = jnp.exp(log_a)
    a_s[...] = a
    u_s[...] = jnp.sqrt(1.0 - a * a) * (i * xc)

    def body(g, hp):
        r0 = pl.multiple_of(g * SUBLANES, SUBLANES)
        a8 = a_s[pl.ds(r0, SUBLANES), :]
        u8 = u_s[pl.ds(r0, SUBLANES), :]
        for s in (1, 2, 4):
            ash = jnp.where(row >= s, pltpu.roll(a8, s, 0), 1.0)
            ush = jnp.where(row >= s, pltpu.roll(u8, s, 0), 0.0)
            u8 = a8 * ush + u8
            a8 = a8 * ash
        h8 = a8 * hp + u8
        h_s[pl.ds(r0, SUBLANES), :] = h8
        return h8[SUBLANES - 1:SUBLANES, :]

    hp = lax.fori_loop(0, tt // SUBLANES, body, hprev_ref[0:1, :])
    hprev_ref[0:1, :] = hp
    o_ref[...] = (h_s[...] * jax.nn.gelu(gb_ref[...])).astype(o_ref.dtype)


def _lru_branch(u, conv_w, conv_b, w_a, b_a, w_x, b_x, lam, *, batch, tp):
    m = u.shape[0]
    width = GROUP_WIDTH
    cw = 1024
    tt = _tile(tp, (640, 384, 256, 128))
    nt, nc = tp // tt, width // cw
    nb = cw // LRU_BLOCK_DIM

    def row_blk(b, j, t):
        return b * nt + t

    in_specs = [
        pl.BlockSpec((tt, cw), lambda b, j, t: (row_blk(b, j, t), j)),
        pl.BlockSpec((SUBLANES, cw),
                     lambda b, j, t: (jnp.maximum(row_blk(b, j, t) * (tt // SUBLANES) - 1, 0), j)),
        pl.BlockSpec((tt, cw), lambda b, j, t: (row_blk(b, j, t), nc + j)),
        pl.BlockSpec((CONV_WIDTH, cw), lambda b, j, t: (0, j)),
        pl.BlockSpec((1, cw), lambda b, j, t: (0, j)),
        pl.BlockSpec((nb, LRU_BLOCK_DIM, LRU_BLOCK_DIM), lambda b, j, t: (j, 0, 0)),
        pl.BlockSpec((1, cw), lambda b, j, t: (0, j)),
        pl.BlockSpec((nb, LRU_BLOCK_DIM, LRU_BLOCK_DIM), lambda b, j, t: (j, 0, 0)),
        pl.BlockSpec((1, cw), lambda b, j, t: (0, j)),
        pl.BlockSpec((1, cw), lambda b, j, t: (0, j)),
    ]
    return pl.pallas_call(
        functools.partial(_lru_kernel, tt=tt, cw=cw),
        grid=(batch, nc, nt),
        in_specs=in_specs,
        out_specs=pl.BlockSpec((tt, cw), lambda b, j, t: (row_blk(b, j, t), j)),
        out_shape=jax.ShapeDtypeStruct((m, width), BF16),
        scratch_shapes=[pltpu.VMEM((tt, cw), F32), pltpu.VMEM((tt, cw), F32), pltpu.VMEM((tt, cw), F32),
                        pltpu.VMEM((SUBLANES, cw), F32)],
        compiler_params=pltpu.CompilerParams(
            dimension_semantics=("parallel", "parallel", "arbitrary"),
            vmem_limit_bytes=_vmem_limit(tt * cw * 4 * 12)),
        name="rglru",
    )(u, u, u, conv_w.astype(F32), conv_b.reshape(1, width).astype(F32),
      w_a.astype(BF16), b_a.reshape(1, width).astype(F32),
      w_x.astype(BF16), b_x.reshape(1, width).astype(F32), lam.reshape(1, width).astype(F32))


def _headnorm_kernel(q_ref, k_ref, v_ref, qg_ref, kg_ref, c_ref, s1_ref, s2_ref, qo_ref, ko_ref, vo_ref,
                     *, q_scale):
    rot = (c_ref[...], s1_ref[...], s2_ref[...])
    qo_ref[...] = _head_norm(q_ref[...], qg_ref[...], q_scale, rot, qo_ref.dtype)
    ko_ref[...] = _head_norm(k_ref[...], kg_ref[...], 1.0, rot, ko_ref.dtype)
    vo_ref[...] = v_ref[...].astype(vo_ref.dtype)


def _headnorm_rot(u, q_gain, k_gain, rot_tables, *, q_scale, tp):
    m = u.shape[0]
    width = 512
    ncb = GROUP_WIDTH // width
    tr = _tile(tp, (640, 384, 256, 128))
    nt = tp // tr
    in_specs = [
        pl.BlockSpec((tr, width), lambda i, j: (i, j)),
        pl.BlockSpec((tr, width), lambda i, j: (i, ncb + j)),
        pl.BlockSpec((tr, width), lambda i, j: (i, 2 * ncb + j)),
        pl.BlockSpec((1, LANES), lambda i, j: (0, 0)),
        pl.BlockSpec((1, LANES), lambda i, j: (0, 0)),
    ] + [pl.BlockSpec((tr, LANES), lambda i, j: (i % nt, 0))] * 3
    out = jax.ShapeDtypeStruct((m, GROUP_WIDTH), BF16)
    return pl.pallas_call(
        functools.partial(_headnorm_kernel, q_scale=q_scale),
        grid=(m // tr, ncb),
        in_specs=in_specs,
        out_specs=[pl.BlockSpec((tr, width), lambda i, j: (i, j))] * 3,
        out_shape=[out, out, out],
        compiler_params=pltpu.CompilerParams(
            dimension_semantics=("parallel", "parallel"),
            vmem_limit_bytes=_vmem_limit(tr * width * 24)),
        name="headnorm_rot",
    )(u, u, u, q_gain.reshape(1, LANES).astype(F32), k_gain.reshape(1, LANES).astype(F32), *rot_tables)


def _rotary_tables(tp):
    inv_freq = ROPE_THETA ** (-jnp.arange(ROT_HALF, dtype=F32) / ROT_HALF)
    ang = jnp.arange(tp, dtype=F32)[:, None] * inv_freq[None, :]
    cos, sin = jnp.cos(ang), jnp.sin(ang)
    ones = jnp.ones((tp, LANES - 2 * ROT_HALF), F32)
    zeros = jnp.zeros((tp, LANES - 2 * ROT_HALF), F32)
    zh = jnp.zeros((tp, ROT_HALF), F32)
    c = jnp.concatenate([cos, cos, ones], axis=1)
    s1 = jnp.concatenate([-sin, zh, zeros], axis=1)
    s2 = jnp.concatenate([zh, sin, zeros], axis=1)
    return c, s1, s2


def _softmax_step(s, v, m_ref, l_ref, acc_ref):
    m_prev = m_ref[...]
    m_new = jnp.maximum(m_prev, jnp.max(s, axis=1, keepdims=True))
    alpha = jnp.exp2(m_prev - m_new)
    p = jnp.exp2(s - m_new)
    if l_ref is not None:
        l_ref[...] = alpha * l_ref[...] + jnp.sum(p, axis=1, keepdims=True)
    acc_ref[...] = alpha * acc_ref[...] + _dot(p.astype(BF16), v)
    m_ref[...] = m_new


def _causal_mask(s, q_ahead):
    row = lax.broadcasted_iota(jnp.int32, s.shape, 0)
    col = lax.broadcasted_iota(jnp.int32, s.shape, 1)
    return jnp.where(col <= row + q_ahead, s, MASK_VALUE)


def _row_chunks(blk, rows):
    if blk % rows:
        return [slice(0, blk)]
    return [slice(r, r + rows) for r in range(0, blk, rows)]


def _kv_sweep(step, qi):
    assert KV_PAIR == 2
    per_trip = 2 * KV_PAIR

    def body(t, carry):
        step(per_trip * t, KV_PAIR, False)
        step(per_trip * t + KV_PAIR, KV_PAIR, False)
        return carry

    lax.fori_loop(0, qi // per_trip, body, 0)
    rem = qi % per_trip
    base = qi - rem

    @pl.when(rem >= KV_PAIR)
    def _():
        step(base, KV_PAIR, False)

    @pl.when(rem % KV_PAIR == 1)
    def _():
        step(qi - 1, KV_PAIR, True)

    @pl.when(rem % KV_PAIR == 0)
    def _():
        step(qi, 1, True)


def _diff_flash_kernel(q1_ref, q2_ref, k1_ref, k2_ref, v_ref, lq1_ref, lk1_ref, lq2_ref, lk2_ref, sg_ref,
                       o_ref, m1, l1, acc1, m2, l2, acc2, *, lambda_init, blk):
    qi = pl.program_id(2)
    sets = ((q1_ref, k1_ref, m1, l1, acc1), (q2_ref, k2_ref, m2, l2, acc2))
    for _, _, m, l, acc in sets:
        m[...] = jnp.full_like(m, MASK_VALUE)
        l[...] = jnp.zeros_like(l)
        acc[...] = jnp.zeros_like(acc)

    def step(j, n, masked):
        r0 = pl.multiple_of(j * blk, blk)
        v = v_ref[pl.ds(r0, n * blk), :]
        for q_ref, k_ref, m, l, acc in sets:
            for rows in _row_chunks(blk, DIFF_ROW_CHUNK):
                s = _nt(q_ref[rows, :], k_ref[pl.ds(r0, n * blk), :])
                if masked:
                    s = _causal_mask(s, (n - 1) * blk + rows.start)
                _softmax_step(s, v, m.at[rows], l.at[rows], acc.at[rows])

    _kv_sweep(step, qi)
    lam = (jnp.exp(jnp.sum(lq1_ref[...] * lk1_ref[...], axis=1, keepdims=True))
           - jnp.exp(jnp.sum(lq2_ref[...] * lk2_ref[...], axis=1, keepdims=True)) + lambda_init)
    o = acc1[...] / l1[...] - lam * (acc2[...] / l2[...])
    o = o * lax.rsqrt(jnp.mean(o * o, axis=-1, keepdims=True) + NORM_EPS)
    o_ref[...] = (o * sg_ref[...] * (1.0 - lambda_init)).astype(o_ref.dtype)


def _diff_attention(qn, kn, vb, lq1, lk1, lq2, lk2, sub_gain, *, lambda_init, batch, tp):
    m = qn.shape[0]
    blk = _tile(tp, (640, 384, 256, 128))
    nq = tp // blk
    hd, vd = DIFF_HEAD_DIM, 2 * DIFF_HEAD_DIM
    vec = lambda p: p.reshape(1, -1).astype(F32)
    pspec = lambda w: pl.BlockSpec((1, w), lambda b, h, i: (0, 0))
    return pl.pallas_call(
        functools.partial(_diff_flash_kernel, lambda_init=lambda_init, blk=blk),
        grid=(batch, DIFF_HEADS, nq),
        in_specs=[
            pl.BlockSpec((blk, hd), lambda b, h, i: (b * nq + i, 2 * h)),
            pl.BlockSpec((blk, hd), lambda b, h, i: (b * nq + i, 2 * h + 1)),
            pl.BlockSpec((tp, hd), lambda b, h, i: (b, 2 * h)),
            pl.BlockSpec((tp, hd), lambda b, h, i: (b, 2 * h + 1)),
            pl.BlockSpec((tp, vd), lambda b, h, i: (b, h)),
            pspec(hd), pspec(hd), pspec(hd), pspec(hd), pspec(vd),
        ],
        out_specs=pl.BlockSpec((blk, vd), lambda b, h, i: (b * nq + i, h)),
        out_shape=jax.ShapeDtypeStruct((m, GROUP_WIDTH), BF16),
        scratch_shapes=[pltpu.VMEM((blk, 1), F32), pltpu.VMEM((blk, 1), F32), pltpu.VMEM((blk, vd), F32),
                        pltpu.VMEM((blk, 1), F32), pltpu.VMEM((blk, 1), F32), pltpu.VMEM((blk, vd), F32)],
        compiler_params=pltpu.CompilerParams(
            dimension_semantics=("parallel", "parallel", "arbitrary"),
            vmem_limit_bytes=_vmem_limit(tp * (2 * hd + vd) * 2 + blk * blk * 4 * 6)),
        name="diff_attention",
    )(qn, qn, kn, kn, vb, vec(lq1), vec(lk1), vec(lq2), vec(lk2), vec(sub_gain))


def _fox_flash_kernel(q_ref, k_ref, v_ref, c_ref, o_ref, vaug_ref, m_ref, acc_ref, *, blk):
    qi = pl.program_id(2)
    hd = FOX_HEAD_DIM

    @pl.when(qi == 0)
    def _():
        vaug_ref[:, 0:hd] = v_ref[...]
        vaug_ref[:, hd:2 * hd] = jnp.ones((vaug_ref.shape[0], hd), vaug_ref.dtype)

    m_ref[...] = jnp.full_like(m_ref, MASK_VALUE)
    acc_ref[...] = jnp.zeros_like(acc_ref)
    cq = c_ref[0, qi][:, 0:1]

    def step(j, n, masked):
        r0 = pl.multiple_of(j * blk, blk)
        ck = jnp.concatenate([c_ref[0, j + i] for i in range(n)], axis=1)
        bias = (cq - ck) * LOG2E
        for rows in _row_chunks(blk, FOX_ROW_CHUNK):
            s = _nt(q_ref[rows, :], k_ref[pl.ds(r0, n * blk), :]) + bias
            if masked:
                s = _causal_mask(s, (n - 1) * blk + rows.start)
            _softmax_step(s, vaug_ref[pl.ds(r0, n * blk), :], m_ref.at[rows], None, acc_ref.at[rows])

    _kv_sweep(step, qi)
    acc = acc_ref[...]
    o_ref[...] = (acc[:, 0:hd] / acc[:, hd:2 * hd]).astype(o_ref.dtype)


def _fox_attention(qn, kn, vb, c_rows, *, batch, tp):
    m = qn.shape[0]
    blk = _tile(tp, (640, 384, 256, 128))
    nq = tp // blk
    hd = FOX_HEAD_DIM
    c_blocks = c_rows.reshape(batch * FOX_HEADS, nq, 1, blk)
    return pl.pallas_call(
        functools.partial(_fox_flash_kernel, blk=blk),
        grid=(batch, FOX_HEADS, nq),
        in_specs=[
            pl.BlockSpec((blk, hd), lambda b, h, i: (b * nq + i, h)),
            pl.BlockSpec((tp, hd), lambda b, h, i: (b, h)),
            pl.BlockSpec((tp, hd), lambda b, h, i: (b, h)),
            pl.BlockSpec((1, nq, 1, blk), lambda b, h, i: (b * FOX_HEADS + h, 0, 0, 0)),
        ],
        out_specs=pl.BlockSpec((blk, hd), lambda b, h, i: (b * nq + i, h)),
        out_shape=jax.ShapeDtypeStruct((m, GROUP_WIDTH), BF16),
        scratch_shapes=[pltpu.VMEM((tp, 2 * hd), BF16), pltpu.VMEM((blk, 1), F32),
                        pltpu.VMEM((blk, 2 * hd), F32)],
        compiler_params=pltpu.CompilerParams(
            dimension_semantics=("parallel", "parallel", "arbitrary"),
            vmem_limit_bytes=_vmem_limit(tp * hd * 2 * 4 + blk * blk * 4 * 6)),
        name="fox_attention",
    )(qn, kn, vb, c_blocks)


def _fox_gate_kernel(hn_ref, wf_ref, bias_ref, c_ref, carry_ref, *, tc):
    t = pl.program_id(1)

    @pl.when(t == 0)
    def _():
        carry_ref[...] = jnp.zeros_like(carry_ref)

    f = _nt(wf_ref[...], hn_ref[...]) + bias_ref[...]
    log_f = jnp.minimum(f, 0.0) - jnp.log1p(jnp.exp(-jnp.abs(f)))
    upper = (lax.broadcasted_iota(jnp.int32, (tc, tc), 0)
             <= lax.broadcasted_iota(jnp.int32, (tc, tc), 1)).astype(BF16)
    hi, mid, lo = _split3(log_f)
    c = _dot(hi, upper) + _dot(mid, upper) + _dot(lo, upper) + carry_ref[:, 0:1]
    c_ref[0] = c
    carry_ref[...] = jnp.broadcast_to(c[:, tc - 1:tc], carry_ref.shape)


def _fox_gates(hn, wf_t, f_bias, *, batch, tp):
    d = hn.shape[1]
    tc = _tile(tp, (640, 384, 256, 128))
    nt = tp // tc
    return pl.pallas_call(
        functools.partial(_fox_gate_kernel, tc=tc),
        grid=(batch, nt),
        in_specs=[pl.BlockSpec((tc, d), lambda b, t: (b * nt + t, 0)),
                  pl.BlockSpec((FOX_HEADS, d), lambda b, t: (0, 0)),
                  pl.BlockSpec((FOX_HEADS, 1), lambda b, t: (0, 0))],
        out_specs=pl.BlockSpec((1, FOX_HEADS, tc), lambda b, t: (b, 0, t)),
        out_shape=jax.ShapeDtypeStruct((batch, FOX_HEADS, tp), F32),
        scratch_shapes=[pltpu.VMEM((FOX_HEADS, LANES), F32)],
        compiler_params=pltpu.CompilerParams(
            dimension_semantics=("parallel", "arbitrary"),
            vmem_limit_bytes=_vmem_limit(tc * d * 2 + tc * tc * 8)),
        name="fox_gates",
    )(hn, wf_t, f_bias.reshape(FOX_HEADS, 1).astype(F32))


def _head_sum(x, ones_bd):
    cols = []
    for g in range(x.shape[1] // LANES):
        hi, lo = _split2(x[:, g * LANES:(g + 1) * LANES])
        cols.append(_dot(hi, ones_bd) + _dot(lo, ones_bd))
    return jnp.concatenate(cols, axis=1)


def _head_ones():
    r = lax.broadcasted_iota(jnp.int32, (LANES, LANES), 0) < RWKV_HEAD_DIM
    c = lax.broadcasted_iota(jnp.int32, (LANES, LANES), 1) < RWKV_HEAD_DIM
    return jnp.where(r == c, 1.0, 0.0).astype(BF16)


def _rwkv_pre_kernel(zr_ref, zrh_ref, zk_ref, zkh_ref, zv_ref, zvh_ref, zl_ref, zlh_ref,
                     mur_ref, muk_ref, muv_ref, mul_ref, w0_ref, wup_ref, a0_ref, aup_ref, gup_ref,
                     kk_ref, ka_ref,
                     r_o, k_o, v_o, a_o, b_o, lw_o, g_o, lora_ref, *, tr, nt):
    first = (pl.program_id(0) % nt) == 0

    def shift(x_ref, h_ref, mu_ref):
        x = x_ref[...]
        rolled = pltpu.roll(x, 1, 0)
        before = jnp.where(first, 0.0, h_ref[SUBLANES - 1:SUBLANES, :])
        row = lax.broadcasted_iota(jnp.int32, (SUBLANES, x.shape[1]), 0)
        top = jnp.where(row == 0, before, rolled[0:SUBLANES, :])
        prev = jnp.concatenate([top, rolled[SUBLANES:, :]], axis=0)
        return x + (prev - x) * mu_ref[...]

    n1, n2 = DECAY_LORA, DECAY_LORA + AAA_LORA

    @pl.when(pl.program_id(1) == 0)
    def _():
        zl = shift(zl_ref, zlh_ref, mul_ref)
        lora_ref[:, 0:n1] = jnp.tanh(zl[:, 0:n1]).astype(BF16)
        lora_ref[:, n1:n2] = zl[:, n1:n2].astype(BF16)
        lora_ref[:, n2:] = _sigmoid(zl[:, n2:]).astype(BF16)

    r = shift(zr_ref, zrh_ref, mur_ref)
    k = shift(zk_ref, zkh_ref, muk_ref)
    v = shift(zv_ref, zvh_ref, muv_ref)
    lw = (-math.exp(-0.5)) * _sigmoid(w0_ref[...] + _dot(lora_ref[:, 0:n1], wup_ref[...]))
    a = _sigmoid(a0_ref[...] + _dot(lora_ref[:, n1:n2], aup_ref[...]))
    g = _dot(lora_ref[:, n2:], gup_ref[...])
    kx = k * kk_ref[...]
    kk = kx * lax.rsqrt(jnp.maximum(_head_sum(kx * kx, _head_ones()), 1e-24))
    r_o[...] = r.astype(r_o.dtype)
    k_o[...] = (k * (1.0 + (a - 1.0) * ka_ref[...])).astype(k_o.dtype)
    v_o[...] = v.astype(v_o.dtype)
    a_o[...] = (-kk).astype(a_o.dtype)
    b_o[...] = (kk * a).astype(b_o.dtype)
    lw_o[...] = lw
    g_o[...] = g.astype(g_o.dtype)


def _rwkv_pre(z, mu, w0, w_up, a0, a_up, g_up, k_k, k_a, *, tp):
    m = z.shape[0]
    width = GROUP_WIDTH
    cw = 512
    ncb = width // cw
    lw = RWKV_LORA_PAD
    tr = _tile(tp, (320, 384, 256, 128))
    nt = tp // tr
    hb = tr // SUBLANES

    def halo(i):
        return jnp.maximum(i * hb - 1, 0)

    mu2 = mu.reshape(1, -1).astype(F32)
    row = lambda p: p.reshape(1, width).astype(F32)
    g_up_p = jnp.zeros((lw - DECAY_LORA - AAA_LORA, width), BF16).at[:GATE_LORA].set(g_up.astype(BF16))
    in_specs = []
    for c in range(3):
        in_specs.append(pl.BlockSpec((tr, cw), lambda i, j, c=c: (i, c * ncb + j)))
        in_specs.append(pl.BlockSpec((SUBLANES, cw), lambda i, j, c=c: (halo(i), c * ncb + j)))
    in_specs.append(pl.BlockSpec((tr, lw), lambda i, j: (i, 3 * width // lw)))
    in_specs.append(pl.BlockSpec((SUBLANES, lw), lambda i, j: (halo(i), 3 * width // lw)))
    for c in range(3):
        in_specs.append(pl.BlockSpec((1, cw), lambda i, j, c=c: (0, c * ncb + j)))
    in_specs.append(pl.BlockSpec((1, lw), lambda i, j: (0, 3 * width // lw)))
    in_specs += [
        pl.BlockSpec((1, cw), lambda i, j: (0, j)),
        pl.BlockSpec((DECAY_LORA, cw), lambda i, j: (0, j)),
        pl.BlockSpec((1, cw), lambda i, j: (0, j)),
        pl.BlockSpec((AAA_LORA, cw), lambda i, j: (0, j)),
        pl.BlockSpec((lw - DECAY_LORA - AAA_LORA, cw), lambda i, j: (0, j)),
        pl.BlockSpec((1, cw), lambda i, j: (0, j)),
        pl.BlockSpec((1, cw), lambda i, j: (0, j)),
    ]
    out16 = jax.ShapeDtypeStruct((m, width), BF16)
    out32 = jax.ShapeDtypeStruct((m, width), F32)
    return pl.pallas_call(
        functools.partial(_rwkv_pre_kernel, tr=tr, nt=nt),
        grid=(m // tr, ncb),
        in_specs=in_specs,
        out_specs=[pl.BlockSpec((tr, cw), lambda i, j: (i, j))] * 7,
        out_shape=[out16] * 5 + [out32, out16],
        scratch_shapes=[pltpu.VMEM((tr, lw), BF16)],
        compiler_params=pltpu.CompilerParams(
            dimension_semantics=("parallel", "arbitrary"),
            vmem_limit_bytes=_vmem_limit(tr * cw * 4 * 16 + tr * lw * 4 * 3)),
        name="rwkv_pre",
    )(z, z, z, z, z, z, z, z, mu2, mu2, mu2, mu2, row(w0), w_up.astype(BF16), row(a0),
      a_up.astype(BF16), g_up_p, row(k_k), row(k_a))


def _apply_and_square(x, r, square, precise):
    n = x.shape[1]
    xh, xl = _split2(x)
    rh, rl = _split2(r)
    if square:
        rh = jnp.concatenate([xh, rh], axis=1)
        rl = jnp.concatenate([xl, rl], axis=1)
    p = _dot(xh, rh)
    if precise:
        p = p + _dot(xh, rl) + _dot(xl, rh)
    if square:
        return p[:, 0:n], r + p[:, n:]
    return x, r + p


def _rwkv_scan_kernel(r_ref, k_ref, v_ref, a_ref, b_ref, lw_ref, y_ref, s_ref, *, tt, pairs):
    C = RWKV_CHUNK
    t = pl.program_id(2)

    @pl.when(t == 0)
    def _():
        s_ref[...] = jnp.zeros_like(s_ref)

    lane_top = lax.broadcasted_iota(jnp.int32, (C, LANES), 1) < RWKV_HEAD_DIM
    ri = lax.broadcasted_iota(jnp.int32, (2 * C, 2 * C), 0)
    ci = lax.broadcasted_iota(jnp.int32, (2 * C, 2 * C), 1)
    strict = (ci & (C - 1)) < (ri & (C - 1))
    incl = (ci & (C - 1)) <= (ri & (C - 1))
    strict2 = jnp.concatenate([strict, strict], axis=1)
    incl2 = jnp.concatenate([incl, incl], axis=1)
    lower = (lax.broadcasted_iota(jnp.int32, (C, C), 1)
             <= lax.broadcasted_iota(jnp.int32, (C, C), 0)).astype(BF16)

    def stack(x):
        return jnp.concatenate([jnp.where(lane_top, x, 0.0), jnp.where(lane_top, 0.0, x)], axis=0)

    def chunk(c, carry):
        r0 = pl.multiple_of(c * C, C)
        P = range(pairs)
        sls = [slice(p * LANES, (p + 1) * LANES) for p in P]
        lw = [lw_ref[pl.ds(r0, C), sl] for sl in sls]
        sp = [_split3(x) for x in lw]
        g_in = [_dot(lower, hi) + _dot(lower, mid) + _dot(lower, lo) for hi, mid, lo in sp]
        g_last = [g[C - 1:C, :] for g in g_in]
        e_in = [jnp.exp(g) for g in g_in]
        e_ex = [jnp.exp(g - l) for g, l in zip(g_in, lw)]
        e_neg = [jnp.exp(-g) for g in g_in]
        e_rem = [jnp.exp(gl - g) for g, gl in zip(g_in, g_last)]
        r = [r_ref[pl.ds(r0, C), sl].astype(F32) for sl in sls]
        k = [k_ref[pl.ds(r0, C), sl].astype(F32) for sl in sls]
        a = [a_ref[pl.ds(r0, C), sl].astype(F32) for sl in sls]
        b = [b_ref[pl.ds(r0, C), sl].astype(F32) for sl in sls]
        xa = [stack(a[p] * e_ex[p]).astype(BF16) for p in P]
        xr = [stack(r[p] * e_in[p]).astype(BF16) for p in P]
        ybk = [jnp.concatenate([stack(b[p] * e_neg[p]), stack(k[p] * e_neg[p])], axis=0).astype(BF16) for p in P]
        vs = [stack(v_ref[pl.ds(r0, C), sl].astype(F32)).astype(BF16) for sl in sls]
        bk = [jnp.concatenate([stack(b[p] * e_rem[p]), stack(k[p] * e_rem[p])], axis=0).astype(BF16) for p in P]
        lp = [jnp.where(strict2, _nt(xa[p], ybk[p]), 0.0) for p in P]
        mp = [jnp.where(incl2, _nt(xr[p], ybk[p]), 0.0).astype(BF16) for p in P]
        s0 = [s_ref[p] for p in P]
        s0b = [s.astype(BF16) for s in s0]
        x = [l[:, 0:2 * C] for l in lp]
        u = [_nt(xa[p], s0b[p]) + _dot(lp[p][:, 2 * C:].astype(BF16), vs[p]) for p in P]
        levels = int(math.log2(C))
        for lvl in range(levels):
            x, u = zip(*[_apply_and_square(xx, uu, lvl < levels - 1, lvl < RWKV_PRECISE_LEVELS)
                         for xx, uu in zip(x, u)])
        uv = [jnp.concatenate([u[p].astype(BF16), vs[p]], axis=0) for p in P]
        ys = [_nt(xr[p], s0b[p]) + _dot(mp[p], uv[p]) for p in P]
        for p in P:
            y_ref[pl.ds(r0, C), sls[p]] = ys[p][0:C, :] + ys[p][C:2 * C, :]
        for p in P:
            s_ref[p] = s0[p] * jnp.exp(g_last[p]) + _tn(uv[p], bk[p])
        return carry

    lax.fori_loop(0, tt // C, chunk, 0)


def _rwkv_scan(r, k, v, a, b, lw, *, batch, tp):
    m = r.shape[0]
    pairs = 8
    cw = pairs * LANES
    tt = _tile(tp, (640, 384, 256, 128))
    nt, nc = tp // tt, GROUP_WIDTH // cw
    spec = pl.BlockSpec((tt, cw), lambda bi, j, t: (bi * nt + t, j))
    return pl.pallas_call(
        functools.partial(_rwkv_scan_kernel, tt=tt, pairs=pairs),
        grid=(batch, nc, nt),
        in_specs=[spec] * 6,
        out_specs=spec,
        out_shape=jax.ShapeDtypeStruct((m, GROUP_WIDTH), F32),
        scratch_shapes=[pltpu.VMEM((pairs, LANES, LANES), F32)],
        compiler_params=pltpu.CompilerParams(
            dimension_semantics=("parallel", "parallel", "arbitrary"),
            vmem_limit_bytes=_vmem_limit(tt * cw * 4 * 8)),
        name="rwkv_scan",
    )(r, k, v, a, b, lw)


def _rwkv_post_kernel(y_ref, r_ref, k_ref, v_ref, g_ref, rk_ref, gw_ref, gb_ref, o_ref):
    ones_bd = _head_ones()
    inv_n = 1.0 / RWKV_HEAD_DIM
    y = y_ref[...]
    mean = _head_sum(y, ones_bd) * inv_n
    d = y - mean
    var = _head_sum(d * d, ones_bd) * inv_n
    yn = d * lax.rsqrt(var + RWKV_GN_EPS) * gw_ref[...] + gb_ref[...]
    rk = r_ref[...].astype(F32) * k_ref[...].astype(F32) * rk_ref[...]
    bonus = _head_sum(rk, ones_bd) * v_ref[...].astype(F32)
    o_ref[...] = ((yn + bonus) * g_ref[...]).astype(o_ref.dtype)


def _rwkv_post(y, r, k, v, g, r_k, gn_w, gn_b, *, tp):
    m = y.shape[0]
    cw = 512
    tr = _tile(tp, (640, 384, 256, 128))
    spec = pl.BlockSpec((tr, cw), lambda i, j: (i, j))
    pspec = pl.BlockSpec((1, cw), lambda i, j: (0, j))
    row = lambda p: p.reshape(1, GROUP_WIDTH).astype(F32)
    return pl.pallas_call(
        _rwkv_post_kernel,
        grid=(m // tr, GROUP_WIDTH // cw),
        in_specs=[spec] * 5 + [pspec] * 3,
        out_specs=spec,
        out_shape=jax.ShapeDtypeStruct((m, GROUP_WIDTH), BF16),
        compiler_params=pltpu.CompilerParams(
            dimension_semantics=("parallel", "parallel"),
            vmem_limit_bytes=_vmem_limit(tr * cw * 4 * 10)),
        name="rwkv_post",
    )(y, r, k, v, g, row(r_k), row(gn_w), row(gn_b))


def _mlp(h, f_gain, f_up, f_down, tm):
    hn = _rmsnorm_rows(h, f_gain)
    hid = _matmul([hn], f_up, n_off=0, n=D_FF, tm=tm, tn=512, out_dtype=BF16, relu2=True, name="ffn_up")
    return _matmul([hid], f_down.astype(BF16), n_off=0, n=D_MODEL, tm=tm, tn=1024, tk=2048,
                   residual=h, name="ffn_down")


def kernel(x, meta_tokens, mix_norm_0, w_in_0, conv_w_0, conv_b_0, lru_wa_0, lru_ba_0, lru_wx_0, lru_bx_0, lru_lam_0, diff_q_gain_0, diff_k_gain_0, diff_lq1_0, diff_lk1_0, diff_lq2_0, diff_lk2_0, diff_sub_gain_0, w_out_0, ffn_norm_0, ffn_up_0, ffn_down_0, mix_norm_1, w_in_1, rwkv_mu_1, rwkv_w0_1, rwkv_w_up_1, rwkv_a0_1, rwkv_a_up_1, rwkv_g_up_1, rwkv_k_k_1, rwkv_k_a_1, rwkv_r_k_1, rwkv_gn_w_1, rwkv_gn_b_1, fox_q_gain_1, fox_k_gain_1, fox_f_bias_1, w_out_1, ffn_norm_1, ffn_up_1, ffn_down_1):
    batch, seq, d = x.shape
    assert d == D_MODEL
    t_real = seq + N_META
    tp = -(-t_real // LANES) * LANES
    m = batch * tp
    tm = _tile(m, (1280, 768, 640, 512, 384, 256, 128))
    gw = GROUP_WIDTH

    meta = jnp.broadcast_to(meta_tokens[None].astype(x.dtype), (batch, N_META, d))
    pad = jnp.zeros((batch, tp - t_real, d), x.dtype)
    h = jnp.concatenate([meta, x, pad], axis=1).reshape(m, d)

    hn = _rmsnorm_rows(h, mix_norm_0)
    u_lru = _matmul([hn], w_in_0, n_off=0, n=2 * gw, tm=tm, tn=512, name="in_proj0_lru")
    u_att = _matmul([hn], w_in_0, n_off=2 * gw, n=3 * gw, tm=tm, tn=512, name="in_proj0_att")
    qn, kn, vb = _headnorm_rot(u_att, diff_q_gain_0, diff_k_gain_0, _rotary_tables(tp),
                               q_scale=DIFF_HEAD_DIM ** -0.5 * LOG2E, tp=tp)
    y_a = _lru_branch(u_lru, conv_w_0, conv_b_0, lru_wa_0, lru_ba_0, lru_wx_0, lru_bx_0, lru_lam_0,
                      batch=batch, tp=tp)
    lambda_init = 0.8 - 0.6 * math.exp(-0.3 * 0)
    y_b = _diff_attention(qn, kn, vb, diff_lq1_0, diff_lk1_0, diff_lq2_0, diff_lk2_0, diff_sub_gain_0,
                          lambda_init=lambda_init, batch=batch, tp=tp)
    h = _matmul([y_a, y_b], w_out_0, n_off=0, n=d, tm=tm, tn=512, residual=h, name="out_proj0")
    h = _mlp(h, ffn_norm_0, ffn_up_0, ffn_down_0, tm)

    hn = _rmsnorm_rows(h, mix_norm_1)
    slab = 3 * gw + DECAY_LORA + AAA_LORA + GATE_LORA
    zw = 3 * gw + RWKV_LORA_PAD
    w_z = jnp.zeros((d, zw), BF16).at[:, :slab].set(w_in_1[:, :slab].astype(BF16))
    w_f = w_in_1[:, slab:slab + 3 * gw].astype(BF16)
    wf_t = w_in_1[:, slab + 3 * gw:].T.astype(BF16)
    mu = jnp.zeros((zw,), F32).at[:slab].set(rwkv_mu_1.astype(F32))
    z = _matmul([hn], w_z, n_off=0, n=zw, tm=tm, tn=512, name="in_proj1_rwkv")
    fq = _matmul([hn], w_f, n_off=0, n=gw, tm=tm, tn=512, out_dtype=BF16,
                 head_norm=(fox_q_gain_1, FOX_HEAD_DIM ** -0.5 * LOG2E), name="in_proj1_fq")
    fk = _matmul([hn], w_f, n_off=gw, n=gw, tm=tm, tn=512, out_dtype=BF16,
                 head_norm=(fox_k_gain_1, 1.0), name="in_proj1_fk")
    fv = _matmul([hn], w_f, n_off=2 * gw, n=gw, tm=tm, tn=512, out_dtype=BF16, name="in_proj1_fv")
    c_rows = _fox_gates(hn, wf_t, fox_f_bias_1, batch=batch, tp=tp)

    r, k32, v, a_neg, b_vec, log_w, g = _rwkv_pre(z, mu, rwkv_w0_1, rwkv_w_up_1, rwkv_a0_1, rwkv_a_up_1,
                                                  rwkv_g_up_1, rwkv_k_k_1, rwkv_k_a_1, tp=tp)
    y = _rwkv_scan(r, k32, v, a_neg, b_vec, log_w, batch=batch, tp=tp)
    y_c = _rwkv_post(y, r, k32, v, g, rwkv_r_k_1, rwkv_gn_w_1, rwkv_gn_b_1, tp=tp)
    y_d = _fox_attention(fq, fk, fv, c_rows, batch=batch, tp=tp)
    h = _matmul([y_c, y_d], w_out_1, n_off=0, n=d, tm=tm, tn=512, residual=h, name="out_proj1")
    h = _mlp(h, ffn_norm_1, ffn_up_1, ffn_down_1, tm)

    return h.reshape(batch, tp, d)[:, N_META:N_META + seq]
```

```python
import functools
import math

import jax
import jax.numpy as jnp
from jax import lax
from jax.experimental import pallas as pl
from jax.experimental.pallas import tpu as pltpu

F32 = jnp.float32
BF16 = jnp.bfloat16

D_MODEL = 4096
N_META = 16
GROUP_WIDTH = D_MODEL // 2
D_FF = 4 * D_MODEL
NORM_EPS = 1e-6
ROPE_THETA = 500000.0

LRU_BLOCK_DIM = 128
CONV_WIDTH = 4
LRU_C = 8.0

DIFF_HEAD_DIM = 128
DIFF_HEADS = GROUP_WIDTH // (2 * DIFF_HEAD_DIM)
ROT_HALF = DIFF_HEAD_DIM // 8

RWKV_HEAD_DIM = 64
RWKV_CHUNK = 64
DECAY_LORA = 128
AAA_LORA = 128
GATE_LORA = 480
RWKV_LORA_PAD = 1024
RWKV_GN_EPS = 64e-5
RWKV_PRECISE_LEVELS = 2

FOX_HEAD_DIM = 128
FOX_HEADS = GROUP_WIDTH // FOX_HEAD_DIM

LANES = 128
SUBLANES = 8
VMEM_CAP_BYTES = 56 * 1024 * 1024
MASK_VALUE = -1e30
LOG2E = math.log2(math.e)
KV_PAIR = 2
DIFF_ROW_CHUNK = 320
FOX_ROW_CHUNK = 128


def _tile(n, prefs):
    for p in prefs:
        if n % p == 0:
            return p
    raise ValueError(f"no tile of {prefs} divides {n}")


def _vmem_limit(block_bytes):
    return int(min(VMEM_CAP_BYTES, 2 * block_bytes + (8 << 20)))


def _nt(a, b):
    return lax.dot_general(a, b, (((1,), (1,)), ((), ())), preferred_element_type=F32)


def _tn(a, b):
    return lax.dot_general(a, b, (((0,), (0,)), ((), ())), preferred_element_type=F32)


def _dot(a, b):
    return jnp.dot(a, b, preferred_element_type=F32)


def _split2(x):
    hi = x.astype(BF16)
    lo = (x - hi.astype(F32)).astype(BF16)
    return hi, lo


def _split3(x):
    hi = x.astype(BF16)
    r1 = x - hi.astype(F32)
    mid = r1.astype(BF16)
    lo = (r1 - mid.astype(F32)).astype(BF16)
    return hi, mid, lo


def _sigmoid(x):
    return 0.5 * jnp.tanh(0.5 * x) + 0.5


def _softplus(x):
    return jnp.maximum(x, 0.0) + jnp.log1p(jnp.exp(-jnp.abs(x)))


def _rmsnorm_kernel(x_ref, g_ref, o_ref):
    x = x_ref[...]
    y = x * lax.rsqrt(jnp.mean(x * x, axis=-1, keepdims=True) + NORM_EPS)
    o_ref[...] = (y * g_ref[...]).astype(o_ref.dtype)


def _rmsnorm_rows(x, gain):
    m, d = x.shape
    tr = _tile(m, (320, 256, 128))
    return pl.pallas_call(
        _rmsnorm_kernel,
        grid=(m // tr,),
        in_specs=[pl.BlockSpec((tr, d), lambda i: (i, 0)),
                  pl.BlockSpec((1, d), lambda i: (0, 0))],
        out_specs=pl.BlockSpec((tr, d), lambda i: (i, 0)),
        out_shape=jax.ShapeDtypeStruct((m, d), BF16),
        compiler_params=pltpu.CompilerParams(
            dimension_semantics=("parallel",),
            vmem_limit_bytes=_vmem_limit(tr * d * 6)),
        name="rmsnorm_rows",
    )(x, gain.reshape(1, d).astype(F32))


def _head_norm(x, gain, scale, rot, out_dtype):
    cols = []
    for g in range(x.shape[1] // LANES):
        xs = x[:, g * LANES:(g + 1) * LANES]
        y = xs * lax.rsqrt(jnp.mean(xs * xs, axis=-1, keepdims=True) + NORM_EPS) * gain
        if rot is not None:
            c, s1, s2 = rot
            y = y * c + pltpu.roll(y, LANES - ROT_HALF, 1) * s1 + pltpu.roll(y, ROT_HALF, 1) * s2
        if scale != 1.0:
            y = y * scale
        cols.append(y.astype(out_dtype))
    return jnp.concatenate(cols, axis=1)


def _mm_kernel(*refs, n_a, k_sizes, nk, has_res, relu2, norm_scale):
    a_refs = refs[:n_a]
    w_ref = refs[n_a]
    pos = n_a + 1
    res_ref = gain_ref = None
    if has_res:
        res_ref = refs[pos]
        pos += 1
    if norm_scale is not None:
        gain_ref = refs[pos]
        pos += 1
    o_ref = refs[pos]
    acc_ref = refs[pos + 1] if nk > 1 else None

    def partial_product():
        acc = None
        off = 0
        for a_ref, ks in zip(a_refs, k_sizes):
            p = _dot(a_ref[...], w_ref[off:off + ks, :].astype(BF16))
            acc = p if acc is None else acc + p
            off += ks
        return acc

    def finish(acc):
        if relu2:
            acc = jnp.square(jnp.maximum(acc, 0.0))
        if has_res:
            acc = acc + res_ref[...]
        if norm_scale is not None:
            o_ref[...] = _head_norm(acc, gain_ref[...], norm_scale, None, o_ref.dtype)
        else:
            o_ref[...] = acc.astype(o_ref.dtype)

    if nk == 1:
        finish(partial_product())
        return

    k = pl.program_id(2)

    @pl.when(k == 0)
    def _():
        acc_ref[...] = jnp.zeros_like(acc_ref)

    acc_ref[...] += partial_product()

    @pl.when(k == nk - 1)
    def _():
        finish(acc_ref[...])


def _matmul(a_list, w, *, n_off, n, tm, tn, tk=None, out_dtype=F32, residual=None, relu2=False,
            head_norm=None, name):
    m = a_list[0].shape[0]
    k_sizes = tuple(a.shape[1] for a in a_list)
    k_total = sum(k_sizes)
    if tk is None:
        tk = k_total
    assert len(a_list) == 1 or tk == k_total
    assert m % tm == 0 and n % tn == 0 and n_off % tn == 0 and k_total % tk == 0
    nk = k_total // tk
    j_off = n_off // tn
    blk_k = (tk,) if len(a_list) == 1 else k_sizes
    in_specs = [pl.BlockSpec((tm, bk), lambda i, j, k: (i, k)) for bk in blk_k]
    in_specs.append(pl.BlockSpec((tk, tn), lambda i, j, k: (k, j + j_off)))
    args = list(a_list) + [w]
    if residual is not None:
        in_specs.append(pl.BlockSpec((tm, tn), lambda i, j, k: (i, j)))
        args.append(residual)
    norm_scale = None
    if head_norm is not None:
        gain, norm_scale = head_norm
        in_specs.append(pl.BlockSpec((1, LANES), lambda i, j, k: (0, 0)))
        args.append(gain.reshape(1, LANES).astype(F32))
    out_bytes = jnp.dtype(out_dtype).itemsize
    block_bytes = (tm * tk * 2 + tk * tn * w.dtype.itemsize + tm * tn * out_bytes
                   + (tm * tn * 4 if residual is not None else 0))
    scratch = [pltpu.VMEM((tm, tn), F32)] if nk > 1 else []
    return pl.pallas_call(
        functools.partial(_mm_kernel, n_a=len(a_list), k_sizes=blk_k, nk=nk,
                          has_res=residual is not None, relu2=relu2, norm_scale=norm_scale),
        grid=(m // tm, n // tn, nk),
        in_specs=in_specs,
        out_specs=pl.BlockSpec((tm, tn), lambda i, j, k: (i, j)),
        out_shape=jax.ShapeDtypeStruct((m, n), out_dtype),
        scratch_shapes=scratch,
        compiler_params=pltpu.CompilerParams(
            dimension_semantics=("parallel", "parallel", "arbitrary"),
            vmem_limit_bytes=_vmem_limit(block_bytes + tm * tn * 2)),
        name=name,
    )(*args)


def _lru_kernel(xb_ref, xh_ref, gb_ref, cw_ref, cb_ref, wa_ref, ba_ref, wx_ref, bx_ref, lam_ref,
                o_ref, a_s, u_s, h_s, hprev_ref, *, tt, cw):
    t = pl.program_id(2)

    @pl.when(t == 0)
    def _():
        hprev_ref[...] = jnp.zeros_like(hprev_ref)

    x = xb_ref[...]
    halo = jnp.where(t == 0, 0.0, xh_ref[...])
    w = cw_ref[...]
    row = lax.broadcasted_iota(jnp.int32, (SUBLANES, cw), 0)
    xc = cb_ref[...] + x * w[CONV_WIDTH - 1:CONV_WIDTH, :]
    for s in range(1, CONV_WIDTH):
        rolled = pltpu.roll(x, s, 0)
        top = jnp.where(row < s, pltpu.roll(halo, s, 0), rolled[0:SUBLANES, :])
        shifted = jnp.concatenate([top, rolled[SUBLANES:, :]], axis=0)
        xc = xc + shifted * w[CONV_WIDTH - 1 - s:CONV_WIDTH - s, :]

    gate_a, gate_x = [], []
    for n in range(cw // LRU_BLOCK_DIM):
        xh = xc[:, n * LRU_BLOCK_DIM:(n + 1) * LRU_BLOCK_DIM].astype(BF16)
        gate_a.append(_dot(xh, wa_ref[n]))
        gate_x.append(_dot(xh, wx_ref[n]))
    r = _sigmoid(jnp.concatenate(gate_a, axis=1) + ba_ref[...])
    i = _sigmoid(jnp.concatenate(gate_x, axis=1) + bx_ref[...])
    log_a = (-LRU_C) * r * _softplus(-lam_ref[...])
    a = jnp.exp(log_a)
    a_s[...] = a
    u_s[...] = jnp.sqrt(1.0 - a * a) * (i * xc)

    def body(g, hp):
        r0 = pl.multiple_of(g * SUBLANES, SUBLANES)
        a8 = a_s[pl.ds(r0, SUBLANES), :]
        u8 = u_s[pl.ds(r0, SUBLANES), :]
        for s in (1, 2, 4):
            ash = jnp.where(row >= s, pltpu.roll(a8, s, 0), 1.0)
            ush = jnp.where(row >= s, pltpu.roll(u8, s, 0), 0.0)
            u8 = a8 * ush + u8
            a8 = a8 * ash
        h8 = a8 * hp + u8
        h_s[pl.ds(r0, SUBLANES), :] = h8
        return h8[SUBLANES - 1:SUBLANES, :]

    hp = lax.fori_loop(0, tt // SUBLANES, body, hprev_ref[0:1, :])
    hprev_ref[0:1, :] = hp
    o_ref[...] = (h_s[...] * jax.nn.gelu(gb_ref[...])).astype(o_ref.dtype)


def _lru_branch(u, conv_w, conv_b, w_a, b_a, w_x, b_x, lam, *, batch, tp):
    m = u.shape[0]
    width = GROUP_WIDTH
    cw = 1024
    tt = _tile(tp, (640, 384, 256, 128))
    nt, nc = tp // tt, width // cw
    nb = cw // LRU_BLOCK_DIM

    def row_blk(b, j, t):
        return b * nt + t

    in_specs = [
        pl.BlockSpec((tt, cw), lambda b, j, t: (row_blk(b, j, t), j)),
        pl.BlockSpec((SUBLANES, cw),
                     lambda b, j, t: (jnp.maximum(row_blk(b, j, t) * (tt // SUBLANES) - 1, 0), j)),
        pl.BlockSpec((tt, cw), lambda b, j, t: (row_blk(b, j, t), nc + j)),
        pl.BlockSpec((CONV_WIDTH, cw), lambda b, j, t: (0, j)),
        pl.BlockSpec((1, cw), lambda b, j, t: (0, j)),
        pl.BlockSpec((nb, LRU_BLOCK_DIM, LRU_BLOCK_DIM), lambda b, j, t: (j, 0, 0)),
        pl.BlockSpec((1, cw), lambda b, j, t: (0, j)),
        pl.BlockSpec((nb, LRU_BLOCK_DIM, LRU_BLOCK_DIM), lambda b, j, t: (j, 0, 0)),
        pl.BlockSpec((1, cw), lambda b, j, t: (0, j)),
        pl.BlockSpec((1, cw), lambda b, j, t: (0, j)),
    ]
    return pl.pallas_call(
        functools.partial(_lru_kernel, tt=tt, cw=cw),
        grid=(batch, nc, nt),
        in_specs=in_specs,
        out_specs=pl.BlockSpec((tt, cw), lambda b, j, t: (row_blk(b, j, t), j)),
        out_shape=jax.ShapeDtypeStruct((m, width), BF16),
        scratch_shapes=[pltpu.VMEM((tt, cw), F32), pltpu.VMEM((tt, cw), F32), pltpu.VMEM((tt, cw), F32),
                        pltpu.VMEM((SUBLANES, cw), F32)],
        compiler_params=pltpu.CompilerParams(
            dimension_semantics=("parallel", "parallel", "arbitrary"),
            vmem_limit_bytes=_vmem_limit(tt * cw * 4 * 12)),
        name="rglru",
    )(u, u, u, conv_w.astype(F32), conv_b.reshape(1, width).astype(F32),
      w_a.astype(BF16), b_a.reshape(1, width).astype(F32),
      w_x.astype(BF16), b_x.reshape(1, width).astype(F32), lam.reshape(1, width).astype(F32))


def _headnorm_kernel(q_ref, k_ref, v_ref, qg_ref, kg_ref, c_ref, s1_ref, s2_ref, qo_ref, ko_ref, vo_ref,
                     *, q_scale):
    rot = (c_ref[...], s1_ref[...], s2_ref[...])
    qo_ref[...] = _head_norm(q_ref[...], qg_ref[...], q_scale, rot, qo_ref.dtype)
    ko_ref[...] = _head_norm(k_ref[...], kg_ref[...], 1.0, rot, ko_ref.dtype)
    vo_ref[...] = v_ref[...].astype(vo_ref.dtype)


def _headnorm_rot(u, q_gain, k_gain, rot_tables, *, q_scale, tp):
    m = u.shape[0]
    width = 512
    ncb = GROUP_WIDTH // width
    tr = _tile(tp, (640, 384, 256, 128))
    nt = tp // tr
    in_specs = [
        pl.BlockSpec((tr, width), lambda i, j: (i, j)),
        pl.BlockSpec((tr, width), lambda i, j: (i, ncb + j)),
        pl.BlockSpec((tr, width), lambda i, j: (i, 2 * ncb + j)),
        pl.BlockSpec((1, LANES), lambda i, j: (0, 0)),
        pl.BlockSpec((1, LANES), lambda i, j: (0, 0)),
    ] + [pl.BlockSpec((tr, LANES), lambda i, j: (i % nt, 0))] * 3
    out = jax.ShapeDtypeStruct((m, GROUP_WIDTH), BF16)
    return pl.pallas_call(
        functools.partial(_headnorm_kernel, q_scale=q_scale),
        grid=(m // tr, ncb),
        in_specs=in_specs,
        out_specs=[pl.BlockSpec((tr, width), lambda i, j: (i, j))] * 3,
        out_shape=[out, out, out],
        compiler_params=pltpu.CompilerParams(
            dimension_semantics=("parallel", "parallel"),
            vmem_limit_bytes=_vmem_limit(tr * width * 24)),
        name="headnorm_rot",
    )(u, u, u, q_gain.reshape(1, LANES).astype(F32), k_gain.reshape(1, LANES).astype(F32), *rot_tables)


def _rotary_tables(tp):
    inv_freq = ROPE_THETA ** (-jnp.arange(ROT_HALF, dtype=F32) / ROT_HALF)
    ang = jnp.arange(tp, dtype=F32)[:, None] * inv_freq[None, :]
    cos, sin = jnp.cos(ang), jnp.sin(ang)
    ones = jnp.ones((tp, LANES - 2 * ROT_HALF), F32)
    zeros = jnp.zeros((tp, LANES - 2 * ROT_HALF), F32)
    zh = jnp.zeros((tp, ROT_HALF), F32)
    c = jnp.concatenate([cos, cos, ones], axis=1)
    s1 = jnp.concatenate([-sin, zh, zeros], axis=1)
    s2 = jnp.concatenate([zh, sin, zeros], axis=1)
    return c, s1, s2


def _softmax_step(s, v, m_ref, l_ref, acc_ref):
    m_prev = m_ref[...]
    m_new = jnp.maximum(m_prev, jnp.max(s, axis=1, keepdims=True))
    alpha = jnp.exp2(m_prev - m_new)
    p = jnp.exp2(s - m_new)
    l_ref[...] = alpha * l_ref[...] + jnp.sum(p, axis=1, keepdims=True)
    acc_ref[...] = alpha * acc_ref[...] + _dot(p.astype(BF16), v)
    m_ref[...] = m_new


def _causal_mask(s, q_ahead):
    row = lax.broadcasted_iota(jnp.int32, s.shape, 0)
    col = lax.broadcasted_iota(jnp.int32, s.shape, 1)
    return jnp.where(col <= row + q_ahead, s, MASK_VALUE)


def _diag_width(last_key):
    return -(-last_key // LANES) * LANES


def _row_chunks(blk, rows):
    if blk % rows:
        return [slice(0, blk)]
    return [slice(r, r + rows) for r in range(0, blk, rows)]


def _kv_sweep(step, qi):
    assert KV_PAIR == 2
    per_trip = 2 * KV_PAIR

    def body(t, carry):
        step(per_trip * t, KV_PAIR, False)
        step(per_trip * t + KV_PAIR, KV_PAIR, False)
        return carry

    lax.fori_loop(0, qi // per_trip, body, 0)
    rem = qi % per_trip
    base = qi - rem

    @pl.when(rem >= KV_PAIR)
    def _():
        step(base, KV_PAIR, False)

    @pl.when(rem % KV_PAIR == 1)
    def _():
        step(qi - 1, KV_PAIR, True)

    @pl.when(rem % KV_PAIR == 0)
    def _():
        step(qi, 1, True)


def _diff_flash_kernel(q1_ref, q2_ref, k1_ref, k2_ref, v_ref, lq1_ref, lk1_ref, lq2_ref, lk2_ref, sg_ref,
                       o_ref, m1, l1, acc1, m2, l2, acc2, *, lambda_init, blk):
    qi = pl.program_id(2)
    sets = ((q1_ref, k1_ref, m1, l1, acc1), (q2_ref, k2_ref, m2, l2, acc2))
    for _, _, m, l, acc in sets:
        m[...] = jnp.full_like(m, MASK_VALUE)
        l[...] = jnp.zeros_like(l)
        acc[...] = jnp.zeros_like(acc)

    def step(j, n, masked):
        r0 = pl.multiple_of(j * blk, blk)
        chunks = _row_chunks(blk, DIFF_ROW_CHUNK)
        if n == 1:
            v = v_ref[pl.ds(r0, blk), :]
            for q_ref, k_ref, m, l, acc in sets:
                for rows in chunks:
                    s = _nt(q_ref[rows, :], k_ref[pl.ds(r0, blk), :])
                    if masked:
                        s = _causal_mask(s, rows.start)
                    _softmax_step(s, v, m.at[rows], l.at[rows], acc.at[rows])
            return
        kws = [_diag_width((n - 1) * blk + rows.stop) if masked else n * blk for rows in chunks]
        work = [(q_ref, k_ref, m, l, acc, rows, kw) for q_ref, k_ref, m, l, acc in sets
                for rows, kw in zip(chunks, kws)]
        ss = [_nt(q_ref[rows, :], k_ref[pl.ds(r0, kw), :]) for q_ref, k_ref, _, _, _, rows, kw in work]
        if masked:
            ss = [_causal_mask(s, (n - 1) * blk + w[5].start) for s, w in zip(ss, work)]
        m_prev = [w[2][w[5], :] for w in work]
        m_new = [jnp.maximum(mp, jnp.max(s, axis=1, keepdims=True)) for mp, s in zip(m_prev, ss)]
        ps = [jnp.exp2(s - mn) for s, mn in zip(ss, m_new)]
        alphas = [jnp.exp2(mp - mn) for mp, mn in zip(m_prev, m_new)]
        sums = [jnp.sum(p, axis=1, keepdims=True) for p in ps]
        pv = [_dot(p.astype(BF16), v_ref[pl.ds(r0, w[6]), :]) for p, w in zip(ps, work)]
        for w, al, x, sm, mn in zip(work, alphas, pv, sums, m_new):
            _, _, m, l, acc, rows, _ = w
            l[rows, :] = al * l[rows, :] + sm
            acc[rows, :] = al * acc[rows, :] + x
            m[rows, :] = mn

    _kv_sweep(step, qi)
    lam = (jnp.exp(jnp.sum(lq1_ref[...] * lk1_ref[...], axis=1, keepdims=True))
           - jnp.exp(jnp.sum(lq2_ref[...] * lk2_ref[...], axis=1, keepdims=True)) + lambda_init)
    o = acc1[...] / l1[...] - lam * (acc2[...] / l2[...])
    o = o * lax.rsqrt(jnp.mean(o * o, axis=-1, keepdims=True) + NORM_EPS)
    o_ref[...] = (o * sg_ref[...] * (1.0 - lambda_init)).astype(o_ref.dtype)


def _diff_attention(qn, kn, vb, lq1, lk1, lq2, lk2, sub_gain, *, lambda_init, batch, tp):
    m = qn.shape[0]
    blk = _tile(tp, (640, 384, 256, 128))
    nq = tp // blk
    hd, vd = DIFF_HEAD_DIM, 2 * DIFF_HEAD_DIM
    vec = lambda p: p.reshape(1, -1).astype(F32)
    pspec = lambda w: pl.BlockSpec((1, w), lambda b, h, i: (0, 0))
    return pl.pallas_call(
        functools.partial(_diff_flash_kernel, lambda_init=lambda_init, blk=blk),
        grid=(batch, DIFF_HEADS, nq),
        in_specs=[
            pl.BlockSpec((blk, hd), lambda b, h, i: (b * nq + i, 2 * h)),
            pl.BlockSpec((blk, hd), lambda b, h, i: (b * nq + i, 2 * h + 1)),
            pl.BlockSpec((tp, hd), lambda b, h, i: (b, 2 * h)),
            pl.BlockSpec((tp, hd), lambda b, h, i: (b, 2 * h + 1)),
            pl.BlockSpec((tp, vd), lambda b, h, i: (b, h)),
            pspec(hd), pspec(hd), pspec(hd), pspec(hd), pspec(vd),
        ],
        out_specs=pl.BlockSpec((blk, vd), lambda b, h, i: (b * nq + i, h)),
        out_shape=jax.ShapeDtypeStruct((m, GROUP_WIDTH), BF16),
        scratch_shapes=[pltpu.VMEM((blk, 1), F32), pltpu.VMEM((blk, 1), F32), pltpu.VMEM((blk, vd), F32),
                        pltpu.VMEM((blk, 1), F32), pltpu.VMEM((blk, 1), F32), pltpu.VMEM((blk, vd), F32)],
        compiler_params=pltpu.CompilerParams(
            dimension_semantics=("parallel", "parallel", "arbitrary"),
            vmem_limit_bytes=_vmem_limit(tp * (2 * hd + vd) * 2 + blk * blk * 4 * 6)),
        name="diff_attention",
    )(qn, qn, kn, kn, vb, vec(lq1), vec(lk1), vec(lq2), vec(lk2), vec(sub_gain))


def _fox_flash_kernel(q_ref, k_ref, v_ref, c_ref, o_ref, vaug_ref, m_ref, acc_ref, *, blk):
    qi = pl.program_id(2)
    hd = FOX_HEAD_DIM

    @pl.when(qi == 0)
    def _():
        vaug_ref[:, 0:hd] = v_ref[...]
        vaug_ref[:, hd:2 * hd] = jnp.ones((vaug_ref.shape[0], hd), vaug_ref.dtype)

    m_ref[...] = jnp.full_like(m_ref, MASK_VALUE)
    acc_ref[...] = jnp.zeros_like(acc_ref)
    cq = c_ref[0, qi][:, 0:1]

    def step(j, n, masked):
        r0 = pl.multiple_of(j * blk, blk)
        ck = jnp.concatenate([c_ref[0, j + i] for i in range(n)], axis=1)
        bias = (cq - ck) * LOG2E
        chunks = _row_chunks(blk, FOX_ROW_CHUNK)
        kws = [_diag_width((n - 1) * blk + rows.stop) if masked else n * blk for rows in chunks]
        ss = [_nt(q_ref[rows, :], k_ref[pl.ds(r0, kw), :]) + bias[:, 0:kw] for rows, kw in zip(chunks, kws)]
        if masked:
            ss = [_causal_mask(s, (n - 1) * blk + rows.start) for s, rows in zip(ss, chunks)]
        m_prev = [m_ref[rows, :] for rows in chunks]
        m_new = [jnp.maximum(mp, jnp.max(s, axis=1, keepdims=True)) for mp, s in zip(m_prev, ss)]
        ps = [jnp.exp2(s - mn).astype(BF16) for s, mn in zip(ss, m_new)]
        alphas = [jnp.exp2(mp - mn) for mp, mn in zip(m_prev, m_new)]
        pv = [_dot(p, vaug_ref[pl.ds(r0, kw), :]) for p, kw in zip(ps, kws)]
        for rows, al, x, mn in zip(chunks, alphas, pv, m_new):
            acc_ref[rows, :] = al * acc_ref[rows, :] + x
            m_ref[rows, :] = mn

    _kv_sweep(step, qi)
    acc = acc_ref[...]
    o_ref[...] = (acc[:, 0:hd] / acc[:, hd:2 * hd]).astype(o_ref.dtype)


def _fox_attention(qn, kn, vb, c_rows, *, batch, tp):
    m = qn.shape[0]
    blk = _tile(tp, (640, 384, 256, 128))
    nq = tp // blk
    hd = FOX_HEAD_DIM
    c_blocks = c_rows.reshape(batch * FOX_HEADS, nq, 1, blk)
    return pl.pallas_call(
        functools.partial(_fox_flash_kernel, blk=blk),
        grid=(batch, FOX_HEADS, nq),
        in_specs=[
            pl.BlockSpec((blk, hd), lambda b, h, i: (b * nq + i, h)),
            pl.BlockSpec((tp, hd), lambda b, h, i: (b, h)),
            pl.BlockSpec((tp, hd), lambda b, h, i: (b, h)),
            pl.BlockSpec((1, nq, 1, blk), lambda b, h, i: (b * FOX_HEADS + h, 0, 0, 0)),
        ],
        out_specs=pl.BlockSpec((blk, hd), lambda b, h, i: (b * nq + i, h)),
        out_shape=jax.ShapeDtypeStruct((m, GROUP_WIDTH), BF16),
        scratch_shapes=[pltpu.VMEM((tp, 2 * hd), BF16), pltpu.VMEM((blk, 1), F32),
                        pltpu.VMEM((blk, 2 * hd), F32)],
        compiler_params=pltpu.CompilerParams(
            dimension_semantics=("parallel", "parallel", "arbitrary"),
            vmem_limit_bytes=_vmem_limit(tp * hd * 2 * 4 + blk * blk * 4 * 6)),
        name="fox_attention",
    )(qn, kn, vb, c_blocks)


def _fox_gate_kernel(hn_ref, wf_ref, bias_ref, c_ref, carry_ref, *, tc):
    t = pl.program_id(1)

    @pl.when(t == 0)
    def _():
        carry_ref[...] = jnp.zeros_like(carry_ref)

    f = _nt(wf_ref[...], hn_ref[...]) + bias_ref[...]
    log_f = jnp.minimum(f, 0.0) - jnp.log1p(jnp.exp(-jnp.abs(f)))
    upper = (lax.broadcasted_iota(jnp.int32, (tc, tc), 0)
             <= lax.broadcasted_iota(jnp.int32, (tc, tc), 1)).astype(BF16)
    hi, mid, lo = _split3(log_f)
    c = _dot(hi, upper) + _dot(mid, upper) + _dot(lo, upper) + carry_ref[:, 0:1]
    c_ref[0] = c
    carry_ref[...] = jnp.broadcast_to(c[:, tc - 1:tc], carry_ref.shape)


def _fox_gates(hn, wf_t, f_bias, *, batch, tp):
    d = hn.shape[1]
    tc = _tile(tp, (640, 384, 256, 128))
    nt = tp // tc
    return pl.pallas_call(
        functools.partial(_fox_gate_kernel, tc=tc),
        grid=(batch, nt),
        in_specs=[pl.BlockSpec((tc, d), lambda b, t: (b * nt + t, 0)),
                  pl.BlockSpec((FOX_HEADS, d), lambda b, t: (0, 0)),
                  pl.BlockSpec((FOX_HEADS, 1), lambda b, t: (0, 0))],
        out_specs=pl.BlockSpec((1, FOX_HEADS, tc), lambda b, t: (b, 0, t)),
        out_shape=jax.ShapeDtypeStruct((batch, FOX_HEADS, tp), F32),
        scratch_shapes=[pltpu.VMEM((FOX_HEADS, LANES), F32)],
        compiler_params=pltpu.CompilerParams(
            dimension_semantics=("parallel", "arbitrary"),
            vmem_limit_bytes=_vmem_limit(tc * d * 2 + tc * tc * 8)),
        name="fox_gates",
    )(hn, wf_t, f_bias.reshape(FOX_HEADS, 1).astype(F32))


def _head_sum(x, ones_bd):
    cols = []
    for g in range(x.shape[1] // LANES):
        hi, lo = _split2(x[:, g * LANES:(g + 1) * LANES])
        cols.append(_dot(hi, ones_bd) + _dot(lo, ones_bd))
    return jnp.concatenate(cols, axis=1)


def _head_ones():
    r = lax.broadcasted_iota(jnp.int32, (LANES, LANES), 0) < RWKV_HEAD_DIM
    c = lax.broadcasted_iota(jnp.int32, (LANES, LANES), 1) < RWKV_HEAD_DIM
    return jnp.where(r == c, 1.0, 0.0).astype(BF16)


def _rwkv_pre_kernel(zr_ref, zrh_ref, zk_ref, zkh_ref, zv_ref, zvh_ref, zl_ref, zlh_ref,
                     mur_ref, muk_ref, muv_ref, mul_ref, w0_ref, wup_ref, a0_ref, aup_ref, gup_ref,
                     kk_ref, ka_ref,
                     r_o, k_o, v_o, a_o, b_o, lw_o, g_o, lora_ref, *, tr, nt):
    first = (pl.program_id(0) % nt) == 0

    def shift(x_ref, h_ref, mu_ref):
        x = x_ref[...]
        rolled = pltpu.roll(x, 1, 0)
        before = jnp.where(first, 0.0, h_ref[SUBLANES - 1:SUBLANES, :])
        row = lax.broadcasted_iota(jnp.int32, (SUBLANES, x.shape[1]), 0)
        top = jnp.where(row == 0, before, rolled[0:SUBLANES, :])
        prev = jnp.concatenate([top, rolled[SUBLANES:, :]], axis=0)
        return x + (prev - x) * mu_ref[...]

    n1, n2 = DECAY_LORA, DECAY_LORA + AAA_LORA

    @pl.when(pl.program_id(1) == 0)
    def _():
        zl = shift(zl_ref, zlh_ref, mul_ref)
        lora_ref[:, 0:n1] = jnp.tanh(zl[:, 0:n1]).astype(BF16)
        lora_ref[:, n1:n2] = zl[:, n1:n2].astype(BF16)
        lora_ref[:, n2:] = _sigmoid(zl[:, n2:]).astype(BF16)

    r = shift(zr_ref, zrh_ref, mur_ref)
    k = shift(zk_ref, zkh_ref, muk_ref)
    v = shift(zv_ref, zvh_ref, muv_ref)
    lw = (-math.exp(-0.5)) * _sigmoid(w0_ref[...] + _dot(lora_ref[:, 0:n1], wup_ref[...]))
    a = _sigmoid(a0_ref[...] + _dot(lora_ref[:, n1:n2], aup_ref[...]))
    g = _dot(lora_ref[:, n2:], gup_ref[...])
    kx = k * kk_ref[...]
    kk = kx * lax.rsqrt(jnp.maximum(_head_sum(kx * kx, _head_ones()), 1e-24))
    r_o[...] = r.astype(r_o.dtype)
    k_o[...] = (k * (1.0 + (a - 1.0) * ka_ref[...])).astype(k_o.dtype)
    v_o[...] = v.astype(v_o.dtype)
    a_o[...] = (-kk).astype(a_o.dtype)
    b_o[...] = (kk * a).astype(b_o.dtype)
    lw_o[...] = lw
    g_o[...] = g.astype(g_o.dtype)


def _rwkv_pre(z, mu, w0, w_up, a0, a_up, g_up, k_k, k_a, *, tp):
    m = z.shape[0]
    width = GROUP_WIDTH
    cw = 512
    ncb = width // cw
    lw = RWKV_LORA_PAD
    tr = _tile(tp, (320, 384, 256, 128))
    nt = tp // tr
    hb = tr // SUBLANES

    def halo(i):
        return jnp.maximum(i * hb - 1, 0)

    mu2 = mu.reshape(1, -1).astype(F32)
    row = lambda p: p.reshape(1, width).astype(F32)
    g_up_p = jnp.zeros((lw - DECAY_LORA - AAA_LORA, width), BF16).at[:GATE_LORA].set(g_up.astype(BF16))
    in_specs = []
    for c in range(3):
        in_specs.append(pl.BlockSpec((tr, cw), lambda i, j, c=c: (i, c * ncb + j)))
        in_specs.append(pl.BlockSpec((SUBLANES, cw), lambda i, j, c=c: (halo(i), c * ncb + j)))
    in_specs.append(pl.BlockSpec((tr, lw), lambda i, j: (i, 3 * width // lw)))
    in_specs.append(pl.BlockSpec((SUBLANES, lw), lambda i, j: (halo(i), 3 * width // lw)))
    for c in range(3):
        in_specs.append(pl.BlockSpec((1, cw), lambda i, j, c=c: (0, c * ncb + j)))
    in_specs.append(pl.BlockSpec((1, lw), lambda i, j: (0, 3 * width // lw)))
    in_specs += [
        pl.BlockSpec((1, cw), lambda i, j: (0, j)),
        pl.BlockSpec((DECAY_LORA, cw), lambda i, j: (0, j)),
        pl.BlockSpec((1, cw), lambda i, j: (0, j)),
        pl.BlockSpec((AAA_LORA, cw), lambda i, j: (0, j)),
        pl.BlockSpec((lw - DECAY_LORA - AAA_LORA, cw), lambda i, j: (0, j)),
        pl.BlockSpec((1, cw), lambda i, j: (0, j)),
        pl.BlockSpec((1, cw), lambda i, j: (0, j)),
    ]
    out16 = jax.ShapeDtypeStruct((m, width), BF16)
    out32 = jax.ShapeDtypeStruct((m, width), F32)
    return pl.pallas_call(
        functools.partial(_rwkv_pre_kernel, tr=tr, nt=nt),
        grid=(m // tr, ncb),
        in_specs=in_specs,
        out_specs=[pl.BlockSpec((tr, cw), lambda i, j: (i, j))] * 7,
        out_shape=[out16] * 5 + [out32, out16],
        scratch_shapes=[pltpu.VMEM((tr, lw), BF16)],
        compiler_params=pltpu.CompilerParams(
            dimension_semantics=("parallel", "arbitrary"),
            vmem_limit_bytes=_vmem_limit(tr * cw * 4 * 16 + tr * lw * 4 * 3)),
        name="rwkv_pre",
    )(z, z, z, z, z, z, z, z, mu2, mu2, mu2, mu2, row(w0), w_up.astype(BF16), row(a0),
      a_up.astype(BF16), g_up_p, row(k_k), row(k_a))


def _apply_and_square(x, r, square, precise):
    n = x.shape[1]
    xh, xl = _split2(x)
    rh, rl = _split2(r)
    if square:
        rh = jnp.concatenate([xh, rh], axis=1)
        rl = jnp.concatenate([xl, rl], axis=1)
    p = _dot(xh, rh)
    if precise:
        p = p + _dot(xh, rl) + _dot(xl, rh)
    if square:
        return p[:, 0:n], r + p[:, n:]
    return x, r + p


def _rwkv_scan_kernel(r_ref, k_ref, v_ref, a_ref, b_ref, lw_ref, y_ref, s_ref, *, tt, pairs):
    C = RWKV_CHUNK
    t = pl.program_id(2)

    @pl.when(t == 0)
    def _():
        s_ref[...] = jnp.zeros_like(s_ref)

    lane_top = lax.broadcasted_iota(jnp.int32, (C, LANES), 1) < RWKV_HEAD_DIM
    ri = lax.broadcasted_iota(jnp.int32, (2 * C, 2 * C), 0)
    ci = lax.broadcasted_iota(jnp.int32, (2 * C, 2 * C), 1)
    strict = (ci & (C - 1)) < (ri & (C - 1))
    incl = (ci & (C - 1)) <= (ri & (C - 1))
    strict2 = jnp.concatenate([strict, strict], axis=1)
    incl2 = jnp.concatenate([incl, incl], axis=1)
    lower = (lax.broadcasted_iota(jnp.int32, (C, C), 1)
             <= lax.broadcasted_iota(jnp.int32, (C, C), 0)).astype(BF16)

    def stack(x):
        return jnp.concatenate([jnp.where(lane_top, x, 0.0), jnp.where(lane_top, 0.0, x)], axis=0)

    def chunk(c, carry):
        r0 = pl.multiple_of(c * C, C)
        P = range(pairs)
        sls = [slice(p * LANES, (p + 1) * LANES) for p in P]
        lw = [lw_ref[pl.ds(r0, C), sl] for sl in sls]
        sp = [_split3(x) for x in lw]
        g_in = [_dot(lower, hi) + _dot(lower, mid) + _dot(lower, lo) for hi, mid, lo in sp]
        g_last = [g[C - 1:C, :] for g in g_in]
        e_in = [jnp.exp(g) for g in g_in]
        e_ex = [jnp.exp(g - l) for g, l in zip(g_in, lw)]
        e_neg = [jnp.exp(-g) for g in g_in]
        e_rem = [jnp.exp(gl - g) for g, gl in zip(g_in, g_last)]
        r = [r_ref[pl.ds(r0, C), sl].astype(F32) for sl in sls]
        k = [k_ref[pl.ds(r0, C), sl].astype(F32) for sl in sls]
        a = [a_ref[pl.ds(r0, C), sl].astype(F32) for sl in sls]
        b = [b_ref[pl.ds(r0, C), sl].astype(F32) for sl in sls]
        xa = [stack(a[p] * e_ex[p]).astype(BF16) for p in P]
        xr = [stack(r[p] * e_in[p]).astype(BF16) for p in P]
        ybk = [jnp.concatenate([stack(b[p] * e_neg[p]), stack(k[p] * e_neg[p])], axis=0).astype(BF16) for p in P]
        vs = [stack(v_ref[pl.ds(r0, C), sl].astype(F32)).astype(BF16) for sl in sls]
        bk = [jnp.concatenate([stack(b[p] * e_rem[p]), stack(k[p] * e_rem[p])], axis=0).astype(BF16) for p in P]
        lp = [jnp.where(strict2, _nt(xa[p], ybk[p]), 0.0) for p in P]
        mp = [jnp.where(incl2, _nt(xr[p], ybk[p]), 0.0).astype(BF16) for p in P]
        s0 = [s_ref[p] for p in P]
        s0b = [s.astype(BF16) for s in s0]
        x = [l[:, 0:2 * C] for l in lp]
        u = [_nt(xa[p], s0b[p]) + _dot(lp[p][:, 2 * C:].astype(BF16), vs[p]) for p in P]
        levels = int(math.log2(C))
        for lvl in range(levels):
            x, u = zip(*[_apply_and_square(xx, uu, lvl < levels - 1, lvl < RWKV_PRECISE_LEVELS)
                         for xx, uu in zip(x, u)])
        uv = [jnp.concatenate([u[p].astype(BF16), vs[p]], axis=0) for p in P]
        ys = [_nt(xr[p], s0b[p]) + _dot(mp[p], uv[p]) for p in P]
        for p in P:
            y_ref[pl.ds(r0, C), sls[p]] = ys[p][0:C, :] + ys[p][C:2 * C, :]
        for p in P:
            s_ref[p] = s0[p] * jnp.exp(g_last[p]) + _tn(uv[p], bk[p])
        return carry

    lax.fori_loop(0, tt // C, chunk, 0)


def _rwkv_scan(r, k, v, a, b, lw, *, batch, tp):
    m = r.shape[0]
    pairs = 8
    cw = pairs * LANES
    tt = _tile(tp, (640, 384, 256, 128))
    nt, nc = tp // tt, GROUP_WIDTH // cw
    spec = pl.BlockSpec((tt, cw), lambda bi, j, t: (bi * nt + t, j))
    return pl.pallas_call(
        functools.partial(_rwkv_scan_kernel, tt=tt, pairs=pairs),
        grid=(batch, nc, nt),
        in_specs=[spec] * 6,
        out_specs=spec,
        out_shape=jax.ShapeDtypeStruct((m, GROUP_WIDTH), F32),
        scratch_shapes=[pltpu.VMEM((pairs, LANES, LANES), F32)],
        compiler_params=pltpu.CompilerParams(
            dimension_semantics=("parallel", "parallel", "arbitrary"),
            vmem_limit_bytes=_vmem_limit(tt * cw * 4 * 8)),
        name="rwkv_scan",
    )(r, k, v, a, b, lw)


def _rwkv_post_kernel(y_ref, r_ref, k_ref, v_ref, g_ref, rk_ref, gw_ref, gb_ref, o_ref):
    ones_bd = _head_ones()
    inv_n = 1.0 / RWKV_HEAD_DIM
    y = y_ref[...]
    mean = _head_sum(y, ones_bd) * inv_n
    d = y - mean
    var = _head_sum(d * d, ones_bd) * inv_n
    yn = d * lax.rsqrt(var + RWKV_GN_EPS) * gw_ref[...] + gb_ref[...]
    rk = r_ref[...].astype(F32) * k_ref[...].astype(F32) * rk_ref[...]
    bonus = _head_sum(rk, ones_bd) * v_ref[...].astype(F32)
    o_ref[...] = ((yn + bonus) * g_ref[...]).astype(o_ref.dtype)


def _rwkv_post(y, r, k, v, g, r_k, gn_w, gn_b, *, tp):
    m = y.shape[0]
    cw = 512
    tr = _tile(tp, (640, 384, 256, 128))
    spec = pl.BlockSpec((tr, cw), lambda i, j: (i, j))
    pspec = pl.BlockSpec((1, cw), lambda i, j: (0, j))
    row = lambda p: p.reshape(1, GROUP_WIDTH).astype(F32)
    return pl.pallas_call(
        _rwkv_post_kernel,
        grid=(m // tr, GROUP_WIDTH // cw),
        in_specs=[spec] * 5 + [pspec] * 3,
        out_specs=spec,
        out_shape=jax.ShapeDtypeStruct((m, GROUP_WIDTH), BF16),
        compiler_params=pltpu.CompilerParams(
            dimension_semantics=("parallel", "parallel"),
            vmem_limit_bytes=_vmem_limit(tr * cw * 4 * 10)),
        name="rwkv_post",
    )(y, r, k, v, g, row(r_k), row(gn_w), row(gn_b))


def _mlp(h, f_gain, f_up, f_down, tm):
    hn = _rmsnorm_rows(h, f_gain)
    hid = _matmul([hn], f_up, n_off=0, n=D_FF, tm=tm, tn=512, out_dtype=BF16, relu2=True, name="ffn_up")
    return _matmul([hid], f_down.astype(BF16), n_off=0, n=D_MODEL, tm=tm, tn=1024, tk=2048,
                   residual=h, name="ffn_down")


def kernel(x, meta_tokens, mix_norm_0, w_in_0, conv_w_0, conv_b_0, lru_wa_0, lru_ba_0, lru_wx_0, lru_bx_0, lru_lam_0, diff_q_gain_0, diff_k_gain_0, diff_lq1_0, diff_lk1_0, diff_lq2_0, diff_lk2_0, diff_sub_gain_0, w_out_0, ffn_norm_0, ffn_up_0, ffn_down_0, mix_norm_1, w_in_1, rwkv_mu_1, rwkv_w0_1, rwkv_w_up_1, rwkv_a0_1, rwkv_a_up_1, rwkv_g_up_1, rwkv_k_k_1, rwkv_k_a_1, rwkv_r_k_1, rwkv_gn_w_1, rwkv_gn_b_1, fox_q_gain_1, fox_k_gain_1, fox_f_bias_1, w_out_1, ffn_norm_1, ffn_up_1, ffn_down_1):
    batch, seq, d = x.shape
    assert d == D_MODEL
    t_real = seq + N_META
    tp = -(-t_real // LANES) * LANES
    m = batch * tp
    tm = _tile(m, (1280, 768, 640, 512, 384, 256, 128))
    gw = GROUP_WIDTH

    meta = jnp.broadcast_to(meta_tokens[None].astype(x.dtype), (batch, N_META, d))
    pad = jnp.zeros((batch, tp - t_real, d), x.dtype)
    h = jnp.concatenate([meta, x, pad], axis=1).reshape(m, d)

    hn = _rmsnorm_rows(h, mix_norm_0)
    u_lru = _matmul([hn], w_in_0, n_off=0, n=2 * gw, tm=tm, tn=512, name="in_proj0_lru")
    u_att = _matmul([hn], w_in_0, n_off=2 * gw, n=3 * gw, tm=tm, tn=512, name="in_proj0_att")
    qn, kn, vb = _headnorm_rot(u_att, diff_q_gain_0, diff_k_gain_0, _rotary_tables(tp),
                               q_scale=DIFF_HEAD_DIM ** -0.5 * LOG2E, tp=tp)
    y_a = _lru_branch(u_lru, conv_w_0, conv_b_0, lru_wa_0, lru_ba_0, lru_wx_0, lru_bx_0, lru_lam_0,
                      batch=batch, tp=tp)
    lambda_init = 0.8 - 0.6 * math.exp(-0.3 * 0)
    y_b = _diff_attention(qn, kn, vb, diff_lq1_0, diff_lk1_0, diff_lq2_0, diff_lk2_0, diff_sub_gain_0,
                          lambda_init=lambda_init, batch=batch, tp=tp)
    h = _matmul([y_a, y_b], w_out_0, n_off=0, n=d, tm=tm, tn=512, residual=h, name="out_proj0")
    h = _mlp(h, ffn_norm_0, ffn_up_0, ffn_down_0, tm)

    hn = _rmsnorm_rows(h, mix_norm_1)
    slab = 3 * gw + DECAY_LORA + AAA_LORA + GATE_LORA
    zw = 3 * gw + RWKV_LORA_PAD
    w_z = jnp.zeros((d, zw), BF16).at[:, :slab].set(w_in_1[:, :slab].astype(BF16))
    w_f = w_in_1[:, slab:slab + 3 * gw].astype(BF16)
    wf_t = w_in_1[:, slab + 3 * gw:].T.astype(BF16)
    mu = jnp.zeros((zw,), F32).at[:slab].set(rwkv_mu_1.astype(F32))
    z = _matmul([hn], w_z, n_off=0, n=zw, tm=tm, tn=512, name="in_proj1_rwkv")
    fq = _matmul([hn], w_f, n_off=0, n=gw, tm=tm, tn=512, out_dtype=BF16,
                 head_norm=(fox_q_gain_1, FOX_HEAD_DIM ** -0.5 * LOG2E), name="in_proj1_fq")
    fk = _matmul([hn], w_f, n_off=gw, n=gw, tm=tm, tn=512, out_dtype=BF16,
                 head_norm=(fox_k_gain_1, 1.0), name="in_proj1_fk")
    fv = _matmul([hn], w_f, n_off=2 * gw, n=gw, tm=tm, tn=512, out_dtype=BF16, name="in_proj1_fv")
    c_rows = _fox_gates(hn, wf_t, fox_f_bias_1, batch=batch, tp=tp)

    r, k32, v, a_neg, b_vec, log_w, g = _rwkv_pre(z, mu, rwkv_w0_1, rwkv_w_up_1, rwkv_a0_1, rwkv_a_up_1,
                                                  rwkv_g_up_1, rwkv_k_k_1, rwkv_k_a_1, tp=tp)
    y = _rwkv_scan(r, k32, v, a_neg, b_vec, log_w, batch=batch, tp=tp)
    y_c = _rwkv_post(y, r, k32, v, g, rwkv_r_k_1, rwkv_gn_w_1, rwkv_gn_b_1, tp=tp)
    y_d = _fox_attention(fq, fk, fv, c_rows, batch=batch, tp=tp)
    h = _matmul([y_c, y_d], w_out_1, n_off=0, n=d, tm=tm, tn=512, residual=h, name="out_proj1")
    h = _mlp(h, ffn_norm_1, ffn_up_1, ffn_down_1, tm)

    return h.reshape(batch, tp, d)[:, N_META:N_META + seq]
```

```python
import functools
import math

import jax
import jax.numpy as jnp
from jax import lax
from jax.experimental import pallas as pl
from jax.experimental.pallas import tpu as pltpu

F32 = jnp.float32
BF16 = jnp.bfloat16

D_MODEL = 4096
N_META = 16
GROUP_WIDTH = D_MODEL // 2
D_FF = 4 * D_MODEL
NORM_EPS = 1e-6
ROPE_THETA = 500000.0

LRU_BLOCK_DIM = 128
CONV_WIDTH = 4
LRU_C = 8.0

DIFF_HEAD_DIM = 128
DIFF_HEADS = GROUP_WIDTH // (2 * DIFF_HEAD_DIM)
ROT_HALF = DIFF_HEAD_DIM // 8

RWKV_HEAD_DIM = 64
RWKV_CHUNK = 64
DECAY_LORA = 128
AAA_LORA = 128
GATE_LORA = 480
RWKV_LORA_PAD = 1024
RWKV_GN_EPS = 64e-5
RWKV_PRECISE_LEVELS = 2

FOX_HEAD_DIM = 128
FOX_HEADS = GROUP_WIDTH // FOX_HEAD_DIM

LANES = 128
SUBLANES = 8
VMEM_CAP_BYTES = 56 * 1024 * 1024
MASK_VALUE = -1e30
LOG2E = math.log2(math.e)
KV_PAIR = 2
DIFF_ROW_CHUNK = 320
FOX_ROW_CHUNK = 128


def _tile(n, prefs):
    for p in prefs:
        if n % p == 0:
            return p
    raise ValueError(f"no tile of {prefs} divides {n}")


def _vmem_limit(block_bytes):
    return int(min(VMEM_CAP_BYTES, 2 * block_bytes + (8 << 20)))


def _nt(a, b):
    return lax.dot_general(a, b, (((1,), (1,)), ((), ())), preferred_element_type=F32)


def _tn(a, b):
    return lax.dot_general(a, b, (((0,), (0,)), ((), ())), preferred_element_type=F32)


def _dot(a, b):
    return jnp.dot(a, b, preferred_element_type=F32)


def _split2(x):
    hi = x.astype(BF16)
    lo = (x - hi.astype(F32)).astype(BF16)
    return hi, lo


def _split3(x):
    hi = x.astype(BF16)
    r1 = x - hi.astype(F32)
    mid = r1.astype(BF16)
    lo = (r1 - mid.astype(F32)).astype(BF16)
    return hi, mid, lo


def _sigmoid(x):
    return 0.5 * jnp.tanh(0.5 * x) + 0.5


def _softplus(x):
    return jnp.maximum(x, 0.0) + jnp.log1p(jnp.exp(-jnp.abs(x)))


def _rmsnorm_kernel(x_ref, g_ref, o_ref):
    x = x_ref[...]
    y = x * lax.rsqrt(jnp.mean(x * x, axis=-1, keepdims=True) + NORM_EPS)
    o_ref[...] = (y * g_ref[...]).astype(o_ref.dtype)


def _rmsnorm_rows(x, gain):
    m, d = x.shape
    tr = _tile(m, (320, 256, 128))
    return pl.pallas_call(
        _rmsnorm_kernel,
        grid=(m // tr,),
        in_specs=[pl.BlockSpec((tr, d), lambda i: (i, 0)),
                  pl.BlockSpec((1, d), lambda i: (0, 0))],
        out_specs=pl.BlockSpec((tr, d), lambda i: (i, 0)),
        out_shape=jax.ShapeDtypeStruct((m, d), BF16),
        compiler_params=pltpu.CompilerParams(
            dimension_semantics=("parallel",),
            vmem_limit_bytes=_vmem_limit(tr * d * 6)),
        name="rmsnorm_rows",
    )(x, gain.reshape(1, d).astype(F32))


def _head_norm(x, gain, scale, rot, out_dtype):
    cols = []
    for g in range(x.shape[1] // LANES):
        xs = x[:, g * LANES:(g + 1) * LANES]
        y = xs * lax.rsqrt(jnp.mean(xs * xs, axis=-1, keepdims=True) + NORM_EPS) * gain
        if rot is not None:
            c, s1, s2 = rot
            y = y * c + pltpu.roll(y, LANES - ROT_HALF, 1) * s1 + pltpu.roll(y, ROT_HALF, 1) * s2
        if scale != 1.0:
            y = y * scale
        cols.append(y.astype(out_dtype))
    return jnp.concatenate(cols, axis=1)


def _mm_kernel(*refs, n_a, k_sizes, nk, has_res, relu2, norm_scale):
    a_refs = refs[:n_a]
    w_ref = refs[n_a]
    pos = n_a + 1
    res_ref = gain_ref = None
    if has_res:
        res_ref = refs[pos]
        pos += 1
    if norm_scale is not None:
        gain_ref = refs[pos]
        pos += 1
    o_ref = refs[pos]
    acc_ref = refs[pos + 1] if nk > 1 else None

    def partial_product():
        acc = None
        off = 0
        for a_ref, ks in zip(a_refs, k_sizes):
            p = _dot(a_ref[...], w_ref[off:off + ks, :].astype(BF16))
            acc = p if acc is None else acc + p
            off += ks
        return acc

    def finish(acc):
        if relu2:
            acc = jnp.square(jnp.maximum(acc, 0.0))
        if has_res:
            acc = acc + res_ref[...]
        if norm_scale is not None:
            o_ref[...] = _head_norm(acc, gain_ref[...], norm_scale, None, o_ref.dtype)
        else:
            o_ref[...] = acc.astype(o_ref.dtype)

    if nk == 1:
        finish(partial_product())
        return

    k = pl.program_id(2)

    @pl.when(k == 0)
    def _():
        acc_ref[...] = jnp.zeros_like(acc_ref)

    acc_ref[...] += partial_product()

    @pl.when(k == nk - 1)
    def _():
        finish(acc_ref[...])


def _matmul(a_list, w, *, n_off, n, tm, tn, tk=None, out_dtype=F32, residual=None, relu2=False,
            head_norm=None, name):
    m = a_list[0].shape[0]
    k_sizes = tuple(a.shape[1] for a in a_list)
    k_total = sum(k_sizes)
    if tk is None:
        tk = k_total
    assert len(a_list) == 1 or tk == k_total
    assert m % tm == 0 and n % tn == 0 and n_off % tn == 0 and k_total % tk == 0
    nk = k_total // tk
    j_off = n_off // tn
    blk_k = (tk,) if len(a_list) == 1 else k_sizes
    in_specs = [pl.BlockSpec((tm, bk), lambda i, j, k: (i, k)) for bk in blk_k]
    in_specs.append(pl.BlockSpec((tk, tn), lambda i, j, k: (k, j + j_off)))
    args = list(a_list) + [w]
    if residual is not None:
        in_specs.append(pl.BlockSpec((tm, tn), lambda i, j, k: (i, j)))
        args.append(residual)
    norm_scale = None
    if head_norm is not None:
        gain, norm_scale = head_norm
        in_specs.append(pl.BlockSpec((1, LANES), lambda i, j, k: (0, 0)))
        args.append(gain.reshape(1, LANES).astype(F32))
    out_bytes = jnp.dtype(out_dtype).itemsize
    block_bytes = (tm * tk * 2 + tk * tn * w.dtype.itemsize + tm * tn * out_bytes
                   + (tm * tn * 4 if residual is not None else 0))
    scratch = [pltpu.VMEM((tm, tn), F32)] if nk > 1 else []
    return pl.pallas_call(
        functools.partial(_mm_kernel, n_a=len(a_list), k_sizes=blk_k, nk=nk,
                          has_res=residual is not None, relu2=relu2, norm_scale=norm_scale),
        grid=(m // tm, n // tn, nk),
        in_specs=in_specs,
        out_specs=pl.BlockSpec((tm, tn), lambda i, j, k: (i, j)),
        out_shape=jax.ShapeDtypeStruct((m, n), out_dtype),
        scratch_shapes=scratch,
        compiler_params=pltpu.CompilerParams(
            dimension_semantics=("parallel", "parallel", "arbitrary"),
            vmem_limit_bytes=_vmem_limit(block_bytes + tm * tn * 2)),
        name=name,
    )(*args)


def _lru_kernel(xb_ref, xh_ref, gb_ref, cw_ref, cb_ref, wa_ref, ba_ref, wx_ref, bx_ref, lam_ref,
                o_ref, a_s, u_s, h_s, hprev_ref, ta_s, tu_s, hprev_t, *, tt, cw):
    t = pl.program_id(2)

    @pl.when(t == 0)
    def _():
        hprev_ref[...] = jnp.zeros_like(hprev_ref)

    x = xb_ref[...]
    halo = jnp.where(t == 0, 0.0, xh_ref[...])
    w = cw_ref[...]
    row = lax.broadcasted_iota(jnp.int32, (SUBLANES, cw), 0)
    xc = cb_ref[...] + x * w[CONV_WIDTH - 1:CONV_WIDTH, :]
    for s in range(1, CONV_WIDTH):
        rolled = pltpu.roll(x, s, 0)
        top = jnp.where(row < s, pltpu.roll(halo, s, 0), rolled[0:SUBLANES, :])
        shifted = jnp.concatenate([top, rolled[SUBLANES:, :]], axis=0)
        xc = xc + shifted * w[CONV_WIDTH - 1 - s:CONV_WIDTH - s, :]

    gate_a, gate_x = [], []
    for n in range(cw // LRU_BLOCK_DIM):
        xh = xc[:, n * LRU_BLOCK_DIM:(n + 1) * LRU_BLOCK_DIM].astype(BF16)
        gate_a.append(_dot(xh, wa_ref[n]))
        gate_x.append(_dot(xh, wx_ref[n]))
    r = _sigmoid(jnp.concatenate(gate_a, axis=1) + ba_ref[...])
    i = _sigmoid(jnp.concatenate(gate_x, axis=1) + bx_ref[...])
    log_a = (-LRU_C) * r * _softplus(-lam_ref[...])
    a = jnp.exp(log_a)
    u = jnp.sqrt(1.0 - a * a) * (i * xc)
    ng = cw // LANES
    for g in range(ng):
        a_s[g] = a[:, g * LANES:(g + 1) * LANES]
        u_s[g] = u[:, g * LANES:(g + 1) * LANES]

    nt8 = tt // SUBLANES

    def rows(ref, g, j):
        return ref[g, pl.ds(j, nt8, stride=SUBLANES), :]

    pas = [[rows(a_s, g, 0) for g in range(ng)]]
    pus = [[rows(u_s, g, 0) for g in range(ng)]]
    for j in range(1, SUBLANES):
        aj = [rows(a_s, g, j) for g in range(ng)]
        uj = [rows(u_s, g, j) for g in range(ng)]
        pus.append([aj[g] * pus[-1][g] + uj[g] for g in range(ng)])
        pas.append([aj[g] * pas[-1][g] for g in range(ng)])
    ta_s[...] = jnp.concatenate(pas[-1], axis=1)
    tu_s[...] = jnp.concatenate(pus[-1], axis=1)

    def body(i, hp):
        hprev_t[pl.ds(i, 1), :] = hp
        return ta_s[pl.ds(i, 1), :] * hp + tu_s[pl.ds(i, 1), :]

    hp = lax.fori_loop(0, nt8, body, hprev_ref[0:1, :])
    hprev_ref[0:1, :] = hp
    hin = hprev_t[...]
    for j in range(SUBLANES):
        for g in range(ng):
            h_s[g, pl.ds(j, nt8, stride=SUBLANES), :] = pas[j][g] * hin[:, g * LANES:(g + 1) * LANES] + pus[j][g]
    h = jnp.concatenate([h_s[g] for g in range(ng)], axis=1)
    o_ref[...] = (h * jax.nn.gelu(gb_ref[...])).astype(o_ref.dtype)


def _lru_branch(u, conv_w, conv_b, w_a, b_a, w_x, b_x, lam, *, batch, tp):
    m = u.shape[0]
    width = GROUP_WIDTH
    cw = 1024
    tt = _tile(tp, (640, 384, 256, 128))
    nt, nc = tp // tt, width // cw
    nb = cw // LRU_BLOCK_DIM

    def row_blk(b, j, t):
        return b * nt + t

    in_specs = [
        pl.BlockSpec((tt, cw), lambda b, j, t: (row_blk(b, j, t), j)),
        pl.BlockSpec((SUBLANES, cw),
                     lambda b, j, t: (jnp.maximum(row_blk(b, j, t) * (tt // SUBLANES) - 1, 0), j)),
        pl.BlockSpec((tt, cw), lambda b, j, t: (row_blk(b, j, t), nc + j)),
        pl.BlockSpec((CONV_WIDTH, cw), lambda b, j, t: (0, j)),
        pl.BlockSpec((1, cw), lambda b, j, t: (0, j)),
        pl.BlockSpec((nb, LRU_BLOCK_DIM, LRU_BLOCK_DIM), lambda b, j, t: (j, 0, 0)),
        pl.BlockSpec((1, cw), lambda b, j, t: (0, j)),
        pl.BlockSpec((nb, LRU_BLOCK_DIM, LRU_BLOCK_DIM), lambda b, j, t: (j, 0, 0)),
        pl.BlockSpec((1, cw), lambda b, j, t: (0, j)),
        pl.BlockSpec((1, cw), lambda b, j, t: (0, j)),
    ]
    return pl.pallas_call(
        functools.partial(_lru_kernel, tt=tt, cw=cw),
        grid=(batch, nc, nt),
        in_specs=in_specs,
        out_specs=pl.BlockSpec((tt, cw), lambda b, j, t: (row_blk(b, j, t), j)),
        out_shape=jax.ShapeDtypeStruct((m, width), BF16),
        scratch_shapes=[pltpu.VMEM((cw // LANES, tt, LANES), F32)] * 3 + [pltpu.VMEM((SUBLANES, cw), F32)]
        + [pltpu.VMEM((tt // SUBLANES, cw), F32)] * 3,
        compiler_params=pltpu.CompilerParams(
            dimension_semantics=("parallel", "parallel", "arbitrary"),
            vmem_limit_bytes=_vmem_limit(tt * cw * 4 * 12)),
        name="rglru",
    )(u, u, u, conv_w.astype(F32), conv_b.reshape(1, width).astype(F32),
      w_a.astype(BF16), b_a.reshape(1, width).astype(F32),
      w_x.astype(BF16), b_x.reshape(1, width).astype(F32), lam.reshape(1, width).astype(F32))


def _headnorm_kernel(q_ref, k_ref, v_ref, qg_ref, kg_ref, c_ref, s1_ref, s2_ref, qo_ref, ko_ref, vo_ref,
                     *, q_scale):
    rot = (c_ref[...], s1_ref[...], s2_ref[...])
    qo_ref[...] = _head_norm(q_ref[...], qg_ref[...], q_scale, rot, qo_ref.dtype)
    ko_ref[...] = _head_norm(k_ref[...], kg_ref[...], 1.0, rot, ko_ref.dtype)
    vo_ref[...] = v_ref[...].astype(vo_ref.dtype)


def _headnorm_rot(u, q_gain, k_gain, rot_tables, *, q_scale, tp):
    m = u.shape[0]
    width = 512
    ncb = GROUP_WIDTH // width
    tr = _tile(tp, (640, 384, 256, 128))
    nt = tp // tr
    in_specs = [
        pl.BlockSpec((tr, width), lambda i, j: (i, j)),
        pl.BlockSpec((tr, width), lambda i, j: (i, ncb + j)),
        pl.BlockSpec((tr, width), lambda i, j: (i, 2 * ncb + j)),
        pl.BlockSpec((1, LANES), lambda i, j: (0, 0)),
        pl.BlockSpec((1, LANES), lambda i, j: (0, 0)),
    ] + [pl.BlockSpec((tr, LANES), lambda i, j: (i % nt, 0))] * 3
    out = jax.ShapeDtypeStruct((m, GROUP_WIDTH), BF16)
    return pl.pallas_call(
        functools.partial(_headnorm_kernel, q_scale=q_scale),
        grid=(m // tr, ncb),
        in_specs=in_specs,
        out_specs=[pl.BlockSpec((tr, width), lambda i, j: (i, j))] * 3,
        out_shape=[out, out, out],
        compiler_params=pltpu.CompilerParams(
            dimension_semantics=("parallel", "parallel"),
            vmem_limit_bytes=_vmem_limit(tr * width * 24)),
        name="headnorm_rot",
    )(u, u, u, q_gain.reshape(1, LANES).astype(F32), k_gain.reshape(1, LANES).astype(F32), *rot_tables)


def _rotary_tables(tp):
    inv_freq = ROPE_THETA ** (-jnp.arange(ROT_HALF, dtype=F32) / ROT_HALF)
    ang = jnp.arange(tp, dtype=F32)[:, None] * inv_freq[None, :]
    cos, sin = jnp.cos(ang), jnp.sin(ang)
    ones = jnp.ones((tp, LANES - 2 * ROT_HALF), F32)
    zeros = jnp.zeros((tp, LANES - 2 * ROT_HALF), F32)
    zh = jnp.zeros((tp, ROT_HALF), F32)
    c = jnp.concatenate([cos, cos, ones], axis=1)
    s1 = jnp.concatenate([-sin, zh, zeros], axis=1)
    s2 = jnp.concatenate([zh, sin, zeros], axis=1)
    return c, s1, s2


def _softmax_step(s, v, m_ref, l_ref, acc_ref):
    m_prev = m_ref[...]
    m_new = jnp.maximum(m_prev, jnp.max(s, axis=1, keepdims=True))
    alpha = jnp.exp2(m_prev - m_new)
    p = jnp.exp2(s - m_new)
    if l_ref is not None:
        l_ref[...] = alpha * l_ref[...] + jnp.sum(p, axis=1, keepdims=True)
    acc_ref[...] = alpha * acc_ref[...] + _dot(p.astype(BF16), v)
    m_ref[...] = m_new


def _causal_mask(s, q_ahead):
    row = lax.broadcasted_iota(jnp.int32, s.shape, 0)
    col = lax.broadcasted_iota(jnp.int32, s.shape, 1)
    return jnp.where(col <= row + q_ahead, s, MASK_VALUE)


def _row_chunks(blk, rows):
    if blk % rows:
        return [slice(0, blk)]
    return [slice(r, r + rows) for r in range(0, blk, rows)]


def _kv_sweep(step, qi):
    assert KV_PAIR == 2
    per_trip = 2 * KV_PAIR

    def body(t, carry):
        step(per_trip * t, KV_PAIR, False)
        step(per_trip * t + KV_PAIR, KV_PAIR, False)
        return carry

    lax.fori_loop(0, qi // per_trip, body, 0)
    rem = qi % per_trip
    base = qi - rem

    @pl.when(rem >= KV_PAIR)
    def _():
        step(base, KV_PAIR, False)

    @pl.when(rem % KV_PAIR == 1)
    def _():
        step(qi - 1, KV_PAIR, True)

    @pl.when(rem % KV_PAIR == 0)
    def _():
        step(qi, 1, True)


def _diff_flash_kernel(q1_ref, q2_ref, k1_ref, k2_ref, v_ref, lq1_ref, lk1_ref, lq2_ref, lk2_ref, sg_ref,
                       o_ref, m1, l1, acc1, m2, l2, acc2, *, lambda_init, blk):
    qi = pl.program_id(2)
    sets = ((q1_ref, k1_ref, m1, l1, acc1), (q2_ref, k2_ref, m2, l2, acc2))
    for _, _, m, l, acc in sets:
        m[...] = jnp.full_like(m, MASK_VALUE)
        l[...] = jnp.zeros_like(l)
        acc[...] = jnp.zeros_like(acc)

    def step(j, n, masked):
        r0 = pl.multiple_of(j * blk, blk)
        v = v_ref[pl.ds(r0, n * blk), :]
        for q_ref, k_ref, m, l, acc in sets:
            for rows in _row_chunks(blk, DIFF_ROW_CHUNK):
                s = _nt(q_ref[rows, :], k_ref[pl.ds(r0, n * blk), :])
                if masked:
                    s = _causal_mask(s, (n - 1) * blk + rows.start)
                _softmax_step(s, v, m.at[rows], l.at[rows], acc.at[rows])

    _kv_sweep(step, qi)
    lam = (jnp.exp(jnp.sum(lq1_ref[...] * lk1_ref[...], axis=1, keepdims=True))
           - jnp.exp(jnp.sum(lq2_ref[...] * lk2_ref[...], axis=1, keepdims=True)) + lambda_init)
    o = acc1[...] / l1[...] - lam * (acc2[...] / l2[...])
    o = o * lax.rsqrt(jnp.mean(o * o, axis=-1, keepdims=True) + NORM_EPS)
    o_ref[...] = (o * sg_ref[...] * (1.0 - lambda_init)).astype(o_ref.dtype)


def _diff_attention(qn, kn, vb, lq1, lk1, lq2, lk2, sub_gain, *, lambda_init, batch, tp):
    m = qn.shape[0]
    blk = _tile(tp, (640, 384, 256, 128))
    nq = tp // blk
    hd, vd = DIFF_HEAD_DIM, 2 * DIFF_HEAD_DIM
    vec = lambda p: p.reshape(1, -1).astype(F32)
    pspec = lambda w: pl.BlockSpec((1, w), lambda b, h, i: (0, 0))
    return pl.pallas_call(
        functools.partial(_diff_flash_kernel, lambda_init=lambda_init, blk=blk),
        grid=(batch, DIFF_HEADS, nq),
        in_specs=[
            pl.BlockSpec((blk, hd), lambda b, h, i: (b * nq + i, 2 * h)),
            pl.BlockSpec((blk, hd), lambda b, h, i: (b * nq + i, 2 * h + 1)),
            pl.BlockSpec((tp, hd), lambda b, h, i: (b, 2 * h)),
            pl.BlockSpec((tp, hd), lambda b, h, i: (b, 2 * h + 1)),
            pl.BlockSpec((tp, vd), lambda b, h, i: (b, h)),
            pspec(hd), pspec(hd), pspec(hd), pspec(hd), pspec(vd),
        ],
        out_specs=pl.BlockSpec((blk, vd), lambda b, h, i: (b * nq + i, h)),
        out_shape=jax.ShapeDtypeStruct((m, GROUP_WIDTH), BF16),
        scratch_shapes=[pltpu.VMEM((blk, 1), F32), pltpu.VMEM((blk, 1), F32), pltpu.VMEM((blk, vd), F32),
                        pltpu.VMEM((blk, 1), F32), pltpu.VMEM((blk, 1), F32), pltpu.VMEM((blk, vd), F32)],
        compiler_params=pltpu.CompilerParams(
            dimension_semantics=("parallel", "parallel", "arbitrary"),
            vmem_limit_bytes=_vmem_limit(tp * (2 * hd + vd) * 2 + blk * blk * 4 * 6)),
        name="diff_attention",
    )(qn, qn, kn, kn, vb, vec(lq1), vec(lk1), vec(lq2), vec(lk2), vec(sub_gain))


def _fox_flash_kernel(q_ref, k_ref, v_ref, c_ref, o_ref, vaug_ref, m_ref, acc_ref, *, blk):
    qi = pl.program_id(2)
    hd = FOX_HEAD_DIM

    @pl.when(qi == 0)
    def _():
        vaug_ref[:, 0:hd] = v_ref[...]
        vaug_ref[:, hd:2 * hd] = jnp.ones((vaug_ref.shape[0], hd), vaug_ref.dtype)

    m_ref[...] = jnp.full_like(m_ref, MASK_VALUE)
    acc_ref[...] = jnp.zeros_like(acc_ref)
    cq = c_ref[0, qi][:, 0:1]

    def step(j, n, masked):
        r0 = pl.multiple_of(j * blk, blk)
        ck = jnp.concatenate([c_ref[0, j + i] for i in range(n)], axis=1)
        bias = (cq - ck) * LOG2E
        for rows in _row_chunks(blk, FOX_ROW_CHUNK):
            s = _nt(q_ref[rows, :], k_ref[pl.ds(r0, n * blk), :]) + bias
            if masked:
                s = _causal_mask(s, (n - 1) * blk + rows.start)
            _softmax_step(s, vaug_ref[pl.ds(r0, n * blk), :], m_ref.at[rows], None, acc_ref.at[rows])

    _kv_sweep(step, qi)
    acc = acc_ref[...]
    o_ref[...] = (acc[:, 0:hd] / acc[:, hd:2 * hd]).astype(o_ref.dtype)


def _fox_attention(qn, kn, vb, c_rows, *, batch, tp):
    m = qn.shape[0]
    blk = _tile(tp, (640, 384, 256, 128))
    nq = tp // blk
    hd = FOX_HEAD_DIM
    c_blocks = c_rows.reshape(batch * FOX_HEADS, nq, 1, blk)
    return pl.pallas_call(
        functools.partial(_fox_flash_kernel, blk=blk),
        grid=(batch, FOX_HEADS, nq),
        in_specs=[
            pl.BlockSpec((blk, hd), lambda b, h, i: (b * nq + i, h)),
            pl.BlockSpec((tp, hd), lambda b, h, i: (b, h)),
            pl.BlockSpec((tp, hd), lambda b, h, i: (b, h)),
            pl.BlockSpec((1, nq, 1, blk), lambda b, h, i: (b * FOX_HEADS + h, 0, 0, 0)),
        ],
        out_specs=pl.BlockSpec((blk, hd), lambda b, h, i: (b * nq + i, h)),
        out_shape=jax.ShapeDtypeStruct((m, GROUP_WIDTH), BF16),
        scratch_shapes=[pltpu.VMEM((tp, 2 * hd), BF16), pltpu.VMEM((blk, 1), F32),
                        pltpu.VMEM((blk, 2 * hd), F32)],
        compiler_params=pltpu.CompilerParams(
            dimension_semantics=("parallel", "parallel", "arbitrary"),
            vmem_limit_bytes=_vmem_limit(tp * hd * 2 * 4 + blk * blk * 4 * 6)),
        name="fox_attention",
    )(qn, kn, vb, c_blocks)


def _fox_gate_kernel(hn_ref, wf_ref, bias_ref, c_ref, carry_ref, *, tc):
    t = pl.program_id(1)

    @pl.when(t == 0)
    def _():
        carry_ref[...] = jnp.zeros_like(carry_ref)

    f = _nt(wf_ref[...], hn_ref[...]) + bias_ref[...]
    log_f = jnp.minimum(f, 0.0) - jnp.log1p(jnp.exp(-jnp.abs(f)))
    upper = (lax.broadcasted_iota(jnp.int32, (tc, tc), 0)
             <= lax.broadcasted_iota(jnp.int32, (tc, tc), 1)).astype(BF16)
    hi, mid, lo = _split3(log_f)
    c = _dot(hi, upper) + _dot(mid, upper) + _dot(lo, upper) + carry_ref[:, 0:1]
    c_ref[0] = c
    carry_ref[...] = jnp.broadcast_to(c[:, tc - 1:tc], carry_ref.shape)


def _fox_gates(hn, wf_t, f_bias, *, batch, tp):
    d = hn.shape[1]
    tc = _tile(tp, (640, 384, 256, 128))
    nt = tp // tc
    return pl.pallas_call(
        functools.partial(_fox_gate_kernel, tc=tc),
        grid=(batch, nt),
        in_specs=[pl.BlockSpec((tc, d), lambda b, t: (b * nt + t, 0)),
                  pl.BlockSpec((FOX_HEADS, d), lambda b, t: (0, 0)),
                  pl.BlockSpec((FOX_HEADS, 1), lambda b, t: (0, 0))],
        out_specs=pl.BlockSpec((1, FOX_HEADS, tc), lambda b, t: (b, 0, t)),
        out_shape=jax.ShapeDtypeStruct((batch, FOX_HEADS, tp), F32),
        scratch_shapes=[pltpu.VMEM((FOX_HEADS, LANES), F32)],
        compiler_params=pltpu.CompilerParams(
            dimension_semantics=("parallel", "arbitrary"),
            vmem_limit_bytes=_vmem_limit(tc * d * 2 + tc * tc * 8)),
        name="fox_gates",
    )(hn, wf_t, f_bias.reshape(FOX_HEADS, 1).astype(F32))


def _head_sum(x, ones_bd):
    cols = []
    for g in range(x.shape[1] // LANES):
        hi, lo = _split2(x[:, g * LANES:(g + 1) * LANES])
        cols.append(_dot(hi, ones_bd) + _dot(lo, ones_bd))
    return jnp.concatenate(cols, axis=1)


def _head_ones():
    r = lax.broadcasted_iota(jnp.int32, (LANES, LANES), 0) < RWKV_HEAD_DIM
    c = lax.broadcasted_iota(jnp.int32, (LANES, LANES), 1) < RWKV_HEAD_DIM
    return jnp.where(r == c, 1.0, 0.0).astype(BF16)


def _rwkv_pre_kernel(zr_ref, zrh_ref, zk_ref, zkh_ref, zv_ref, zvh_ref, zl_ref, zlh_ref,
                     mur_ref, muk_ref, muv_ref, mul_ref, w0_ref, wup_ref, a0_ref, aup_ref, gup_ref,
                     kk_ref, ka_ref,
                     r_o, k_o, v_o, a_o, b_o, lw_o, g_o, lora_ref, *, tr, nt):
    first = (pl.program_id(0) % nt) == 0

    def shift(x_ref, h_ref, mu_ref):
        x = x_ref[...]
        rolled = pltpu.roll(x, 1, 0)
        before = jnp.where(first, 0.0, h_ref[SUBLANES - 1:SUBLANES, :])
        row = lax.broadcasted_iota(jnp.int32, (SUBLANES, x.shape[1]), 0)
        top = jnp.where(row == 0, before, rolled[0:SUBLANES, :])
        prev = jnp.concatenate([top, rolled[SUBLANES:, :]], axis=0)
        return x + (prev - x) * mu_ref[...]

    n1, n2 = DECAY_LORA, DECAY_LORA + AAA_LORA

    @pl.when(pl.program_id(1) == 0)
    def _():
        zl = shift(zl_ref, zlh_ref, mul_ref)
        lora_ref[:, 0:n1] = jnp.tanh(zl[:, 0:n1]).astype(BF16)
        lora_ref[:, n1:n2] = zl[:, n1:n2].astype(BF16)
        lora_ref[:, n2:] = _sigmoid(zl[:, n2:]).astype(BF16)

    r = shift(zr_ref, zrh_ref, mur_ref)
    k = shift(zk_ref, zkh_ref, muk_ref)
    v = shift(zv_ref, zvh_ref, muv_ref)
    lw = (-math.exp(-0.5)) * _sigmoid(w0_ref[...] + _dot(lora_ref[:, 0:n1], wup_ref[...]))
    a = _sigmoid(a0_ref[...] + _dot(lora_ref[:, n1:n2], aup_ref[...]))
    g = _dot(lora_ref[:, n2:], gup_ref[...])
    kx = k * kk_ref[...]
    kk = kx * lax.rsqrt(jnp.maximum(_head_sum(kx * kx, _head_ones()), 1e-24))
    r_o[...] = r.astype(r_o.dtype)
    k_o[...] = (k * (1.0 + (a - 1.0) * ka_ref[...])).astype(k_o.dtype)
    v_o[...] = v.astype(v_o.dtype)
    a_o[...] = (-kk).astype(a_o.dtype)
    b_o[...] = (kk * a).astype(b_o.dtype)
    lw_o[...] = lw
    g_o[...] = g.astype(g_o.dtype)


def _rwkv_pre(z, mu, w0, w_up, a0, a_up, g_up, k_k, k_a, *, tp):
    m = z.shape[0]
    width = GROUP_WIDTH
    cw = 512
    ncb = width // cw
    lw = RWKV_LORA_PAD
    tr = _tile(tp, (320, 384, 256, 128))
    nt = tp // tr
    hb = tr // SUBLANES

    def halo(i):
        return jnp.maximum(i * hb - 1, 0)

    mu2 = mu.reshape(1, -1).astype(F32)
    row = lambda p: p.reshape(1, width).astype(F32)
    g_up_p = jnp.zeros((lw - DECAY_LORA - AAA_LORA, width), BF16).at[:GATE_LORA].set(g_up.astype(BF16))
    in_specs = []
    for c in range(3):
        in_specs.append(pl.BlockSpec((tr, cw), lambda i, j, c=c: (i, c * ncb + j)))
        in_specs.append(pl.BlockSpec((SUBLANES, cw), lambda i, j, c=c: (halo(i), c * ncb + j)))
    in_specs.append(pl.BlockSpec((tr, lw), lambda i, j: (i, 3 * width // lw)))
    in_specs.append(pl.BlockSpec((SUBLANES, lw), lambda i, j: (halo(i), 3 * width // lw)))
    for c in range(3):
        in_specs.append(pl.BlockSpec((1, cw), lambda i, j, c=c: (0, c * ncb + j)))
    in_specs.append(pl.BlockSpec((1, lw), lambda i, j: (0, 3 * width // lw)))
    in_specs += [
        pl.BlockSpec((1, cw), lambda i, j: (0, j)),
        pl.BlockSpec((DECAY_LORA, cw), lambda i, j: (0, j)),
        pl.BlockSpec((1, cw), lambda i, j: (0, j)),
        pl.BlockSpec((AAA_LORA, cw), lambda i, j: (0, j)),
        pl.BlockSpec((lw - DECAY_LORA - AAA_LORA, cw), lambda i, j: (0, j)),
        pl.BlockSpec((1, cw), lambda i, j: (0, j)),
        pl.BlockSpec((1, cw), lambda i, j: (0, j)),
    ]
    out16 = jax.ShapeDtypeStruct((m, width), BF16)
    out32 = jax.ShapeDtypeStruct((m, width), F32)
    return pl.pallas_call(
        functools.partial(_rwkv_pre_kernel, tr=tr, nt=nt),
        grid=(m // tr, ncb),
        in_specs=in_specs,
        out_specs=[pl.BlockSpec((tr, cw), lambda i, j: (i, j))] * 7,
        out_shape=[out16] * 5 + [out32, out16],
        scratch_shapes=[pltpu.VMEM((tr, lw), BF16)],
        compiler_params=pltpu.CompilerParams(
            dimension_semantics=("parallel", "arbitrary"),
            vmem_limit_bytes=_vmem_limit(tr * cw * 4 * 16 + tr * lw * 4 * 3)),
        name="rwkv_pre",
    )(z, z, z, z, z, z, z, z, mu2, mu2, mu2, mu2, row(w0), w_up.astype(BF16), row(a0),
      a_up.astype(BF16), g_up_p, row(k_k), row(k_a))


def _apply_and_square(x, r, square, precise):
    n = x.shape[1]
    xh, xl = _split2(x)
    rh, rl = _split2(r)
    if square:
        rh = jnp.concatenate([xh, rh], axis=1)
        rl = jnp.concatenate([xl, rl], axis=1)
    p = _dot(xh, rh)
    if precise:
        p = p + _dot(xh, rl) + _dot(xl, rh)
    if square:
        return p[:, 0:n], r + p[:, n:]
    return x, r + p


def _rwkv_scan_kernel(r_ref, k_ref, v_ref, a_ref, b_ref, lw_ref, y_ref, s_ref, *, tt, pairs):
    C = RWKV_CHUNK
    t = pl.program_id(2)

    @pl.when(t == 0)
    def _():
        s_ref[...] = jnp.zeros_like(s_ref)

    lane_top = lax.broadcasted_iota(jnp.int32, (C, LANES), 1) < RWKV_HEAD_DIM
    ri = lax.broadcasted_iota(jnp.int32, (2 * C, 2 * C), 0)
    ci = lax.broadcasted_iota(jnp.int32, (2 * C, 2 * C), 1)
    strict = (ci & (C - 1)) < (ri & (C - 1))
    incl = (ci & (C - 1)) <= (ri & (C - 1))
    strict2 = jnp.concatenate([strict, strict], axis=1)
    incl2 = jnp.concatenate([incl, incl], axis=1)
    lower = (lax.broadcasted_iota(jnp.int32, (C, C), 1)
             <= lax.broadcasted_iota(jnp.int32, (C, C), 0)).astype(BF16)

    def stack(x):
        return jnp.concatenate([jnp.where(lane_top, x, 0.0), jnp.where(lane_top, 0.0, x)], axis=0)

    def chunk(c, carry):
        r0 = pl.multiple_of(c * C, C)
        P = range(pairs)
        sls = [slice(p * LANES, (p + 1) * LANES) for p in P]
        lw = [lw_ref[pl.ds(r0, C), sl] for sl in sls]
        sp = [_split3(x) for x in lw]
        g_in = [_dot(lower, hi) + _dot(lower, mid) + _dot(lower, lo) for hi, mid, lo in sp]
        g_last = [g[C - 1:C, :] for g in g_in]
        e_in = [jnp.exp(g) for g in g_in]
        e_ex = [jnp.exp(g - l) for g, l in zip(g_in, lw)]
        e_neg = [jnp.exp(-g) for g in g_in]
        e_rem = [jnp.exp(gl - g) for g, gl in zip(g_in, g_last)]
        r = [r_ref[pl.ds(r0, C), sl].astype(F32) for sl in sls]
        k = [k_ref[pl.ds(r0, C), sl].astype(F32) for sl in sls]
        a = [a_ref[pl.ds(r0, C), sl].astype(F32) for sl in sls]
        b = [b_ref[pl.ds(r0, C), sl].astype(F32) for sl in sls]
        xa = [stack(a[p] * e_ex[p]).astype(BF16) for p in P]
        xr = [stack(r[p] * e_in[p]).astype(BF16) for p in P]
        ybk = [jnp.concatenate([stack(b[p] * e_neg[p]), stack(k[p] * e_neg[p])], axis=0).astype(BF16) for p in P]
        vs = [stack(v_ref[pl.ds(r0, C), sl].astype(F32)).astype(BF16) for sl in sls]
        bk = [jnp.concatenate([stack(b[p] * e_rem[p]), stack(k[p] * e_rem[p])], axis=0).astype(BF16) for p in P]
        lp = [jnp.where(strict2, _nt(xa[p], ybk[p]), 0.0) for p in P]
        mp = [jnp.where(incl2, _nt(xr[p], ybk[p]), 0.0).astype(BF16) for p in P]
        s0 = [s_ref[p] for p in P]
        s0b = [s.astype(BF16) for s in s0]
        x = [l[:, 0:2 * C] for l in lp]
        u = [_nt(xa[p], s0b[p]) + _dot(lp[p][:, 2 * C:].astype(BF16), vs[p]) for p in P]
        levels = int(math.log2(C))
        for lvl in range(levels):
            x, u = zip(*[_apply_and_square(xx, uu, lvl < levels - 1, lvl < RWKV_PRECISE_LEVELS)
                         for xx, uu in zip(x, u)])
        uv = [jnp.concatenate([u[p].astype(BF16), vs[p]], axis=0) for p in P]
        ys = [_nt(xr[p], s0b[p]) + _dot(mp[p], uv[p]) for p in P]
        for p in P:
            y_ref[pl.ds(r0, C), sls[p]] = ys[p][0:C, :] + ys[p][C:2 * C, :]
        for p in P:
            s_ref[p] = s0[p] * jnp.exp(g_last[p]) + _tn(uv[p], bk[p])
        return carry

    lax.fori_loop(0, tt // C, chunk, 0)


def _rwkv_scan(r, k, v, a, b, lw, *, batch, tp):
    m = r.shape[0]
    pairs = 8
    cw = pairs * LANES
    tt = _tile(tp, (640, 384, 256, 128))
    nt, nc = tp // tt, GROUP_WIDTH // cw
    spec = pl.BlockSpec((tt, cw), lambda bi, j, t: (bi * nt + t, j))
    return pl.pallas_call(
        functools.partial(_rwkv_scan_kernel, tt=tt, pairs=pairs),
        grid=(batch, nc, nt),
        in_specs=[spec] * 6,
        out_specs=spec,
        out_shape=jax.ShapeDtypeStruct((m, GROUP_WIDTH), F32),
        scratch_shapes=[pltpu.VMEM((pairs, LANES, LANES), F32)],
        compiler_params=pltpu.CompilerParams(
            dimension_semantics=("parallel", "parallel", "arbitrary"),
            vmem_limit_bytes=_vmem_limit(tt * cw * 4 * 8)),
        name="rwkv_scan",
    )(r, k, v, a, b, lw)


def _rwkv_post_kernel(y_ref, r_ref, k_ref, v_ref, g_ref, rk_ref, gw_ref, gb_ref, o_ref):
    ones_bd = _head_ones()
    inv_n = 1.0 / RWKV_HEAD_DIM
    y = y_ref[...]
    mean = _head_sum(y, ones_bd) * inv_n
    d = y - mean
    var = _head_sum(d * d, ones_bd) * inv_n
    yn = d * lax.rsqrt(var + RWKV_GN_EPS) * gw_ref[...] + gb_ref[...]
    rk = r_ref[...].astype(F32) * k_ref[...].astype(F32) * rk_ref[...]
    bonus = _head_sum(rk, ones_bd) * v_ref[...].astype(F32)
    o_ref[...] = ((yn + bonus) * g_ref[...]).astype(o_ref.dtype)


def _rwkv_post(y, r, k, v, g, r_k, gn_w, gn_b, *, tp):
    m = y.shape[0]
    cw = 512
    tr = _tile(tp, (640, 384, 256, 128))
    spec = pl.BlockSpec((tr, cw), lambda i, j: (i, j))
    pspec = pl.BlockSpec((1, cw), lambda i, j: (0, j))
    row = lambda p: p.reshape(1, GROUP_WIDTH).astype(F32)
    return pl.pallas_call(
        _rwkv_post_kernel,
        grid=(m // tr, GROUP_WIDTH // cw),
        in_specs=[spec] * 5 + [pspec] * 3,
        out_specs=spec,
        out_shape=jax.ShapeDtypeStruct((m, GROUP_WIDTH), BF16),
        compiler_params=pltpu.CompilerParams(
            dimension_semantics=("parallel", "parallel"),
            vmem_limit_bytes=_vmem_limit(tr * cw * 4 * 10)),
        name="rwkv_post",
    )(y, r, k, v, g, row(r_k), row(gn_w), row(gn_b))


def _mlp(h, f_gain, f_up, f_down, tm):
    hn = _rmsnorm_rows(h, f_gain)
    hid = _matmul([hn], f_up, n_off=0, n=D_FF, tm=tm, tn=512, out_dtype=BF16, relu2=True, name="ffn_up")
    return _matmul([hid], f_down.astype(BF16), n_off=0, n=D_MODEL, tm=tm, tn=1024, tk=2048,
                   residual=h, name="ffn_down")


def kernel(x, meta_tokens, mix_norm_0, w_in_0, conv_w_0, conv_b_0, lru_wa_0, lru_ba_0, lru_wx_0, lru_bx_0, lru_lam_0, diff_q_gain_0, diff_k_gain_0, diff_lq1_0, diff_lk1_0, diff_lq2_0, diff_lk2_0, diff_sub_gain_0, w_out_0, ffn_norm_0, ffn_up_0, ffn_down_0, mix_norm_1, w_in_1, rwkv_mu_1, rwkv_w0_1, rwkv_w_up_1, rwkv_a0_1, rwkv_a_up_1, rwkv_g_up_1, rwkv_k_k_1, rwkv_k_a_1, rwkv_r_k_1, rwkv_gn_w_1, rwkv_gn_b_1, fox_q_gain_1, fox_k_gain_1, fox_f_bias_1, w_out_1, ffn_norm_1, ffn_up_1, ffn_down_1):
    batch, seq, d = x.shape
    assert d == D_MODEL
    t_real = seq + N_META
    tp = -(-t_real // LANES) * LANES
    m = batch * tp
    tm = _tile(m, (1280, 768, 640, 512, 384, 256, 128))
    gw = GROUP_WIDTH

    meta = jnp.broadcast_to(meta_tokens[None].astype(x.dtype), (batch, N_META, d))
    pad = jnp.zeros((batch, tp - t_real, d), x.dtype)
    h = jnp.concatenate([meta, x, pad], axis=1).reshape(m, d)

    hn = _rmsnorm_rows(h, mix_norm_0)
    u_lru = _matmul([hn], w_in_0, n_off=0, n=2 * gw, tm=tm, tn=512, name="in_proj0_lru")
    u_att = _matmul([hn], w_in_0, n_off=2 * gw, n=3 * gw, tm=tm, tn=512, name="in_proj0_att")
    qn, kn, vb = _headnorm_rot(u_att, diff_q_gain_0, diff_k_gain_0, _rotary_tables(tp),
                               q_scale=DIFF_HEAD_DIM ** -0.5 * LOG2E, tp=tp)
    y_a = _lru_branch(u_lru, conv_w_0, conv_b_0, lru_wa_0, lru_ba_0, lru_wx_0, lru_bx_0, lru_lam_0,
                      batch=batch, tp=tp)
    lambda_init = 0.8 - 0.6 * math.exp(-0.3 * 0)
    y_b = _diff_attention(qn, kn, vb, diff_lq1_0, diff_lk1_0, diff_lq2_0, diff_lk2_0, diff_sub_gain_0,
                          lambda_init=lambda_init, batch=batch, tp=tp)
    h = _matmul([y_a, y_b], w_out_0, n_off=0, n=d, tm=tm, tn=512, residual=h, name="out_proj0")
    h = _mlp(h, ffn_norm_0, ffn_up_0, ffn_down_0, tm)

    hn = _rmsnorm_rows(h, mix_norm_1)
    slab = 3 * gw + DECAY_LORA + AAA_LORA + GATE_LORA
    zw = 3 * gw + RWKV_LORA_PAD
    w_z = jnp.zeros((d, zw), BF16).at[:, :slab].set(w_in_1[:, :slab].astype(BF16))
    w_f = w_in_1[:, slab:slab + 3 * gw].astype(BF16)
    wf_t = w_in_1[:, slab + 3 * gw:].T.astype(BF16)
    mu = jnp.zeros((zw,), F32).at[:slab].set(rwkv_mu_1.astype(F32))
    z = _matmul([hn], w_z, n_off=0, n=zw, tm=tm, tn=512, name="in_proj1_rwkv")
    fq = _matmul([hn], w_f, n_off=0, n=gw, tm=tm, tn=512, out_dtype=BF16,
                 head_norm=(fox_q_gain_1, FOX_HEAD_DIM ** -0.5 * LOG2E), name="in_proj1_fq")
    fk = _matmul([hn], w_f, n_off=gw, n=gw, tm=tm, tn=512, out_dtype=BF16,
                 head_norm=(fox_k_gain_1, 1.0), name="in_proj1_fk")
    fv = _matmul([hn], w_f, n_off=2 * gw, n=gw, tm=tm, tn=512, out_dtype=BF16, name="in_proj1_fv")
    c_rows = _fox_gates(hn, wf_t, fox_f_bias_1, batch=batch, tp=tp)

    r, k32, v, a_neg, b_vec, log_w, g = _rwkv_pre(z, mu, rwkv_w0_1, rwkv_w_up_1, rwkv_a0_1, rwkv_a_up_1,
                                                  rwkv_g_up_1, rwkv_k_k_1, rwkv_k_a_1, tp=tp)
    y = _rwkv_scan(r, k32, v, a_neg, b_vec, log_w, batch=batch, tp=tp)
    y_c = _rwkv_post(y, r, k32, v, g, rwkv_r_k_1, rwkv_gn_w_1, rwkv_gn_b_1, tp=tp)
    y_d = _fox_attention(fq, fk, fv, c_rows, batch=batch, tp=tp)
    h = _matmul([y_c, y_d], w_out_1, n_off=0, n=d, tm=tm, tn=512, residual=h, name="out_proj1")
    h = _mlp(h, ffn_norm_1, ffn_up_1, ffn_down_1, tm)

    return h.reshape(batch, tp, d)[:, N_META:N_META + seq]
```
